```python
import jax, jax.numpy as jnp
from jax import lax
import numpy as np

D_MODEL = 4096
BATCH = 8
SEQ = 2048
DEPTH = 2
DEC_BATCH = 8
DEC_SEQ = 32
PAST_LEN = 2048

CHUNK = 64
MIX_WIDTH = D_MODEL
HGRN_WIDTH = MIX_WIDTH // 2
HGRN_DK = 128
HGRN_HEADS = HGRN_WIDTH // HGRN_DK
HGRN_DV = HGRN_WIDTH // HGRN_HEADS
GLA_WIDTH = MIX_WIDTH - HGRN_WIDTH
GLA_HEADS = 4
GLA_KEY_WIDTH = GLA_WIDTH // 2
GLA_DK = GLA_KEY_WIDTH // GLA_HEADS
GLA_DV = GLA_WIDTH // GLA_HEADS
GLA_LOW_RANK = 16
GLA_GATE_NORMALIZER = 16.0
_IN_SIZES = (HGRN_WIDTH, HGRN_WIDTH, HGRN_WIDTH, HGRN_WIDTH,
             GLA_KEY_WIDTH, GLA_KEY_WIDTH, GLA_WIDTH, GLA_WIDTH, GLA_LOW_RANK)
IN_COLS = sum(_IN_SIZES)
PEER_HEADS = 8
PEER_DK = 256
N_KEYS = 128
N_EXPERTS = N_KEYS * N_KEYS
PEER_TOPK = 16
PEER_BLOCK = 64
PLE_DIM = 256
EPS = 1e-6

kernel_name = "hymba_hgrn2_gla_peer_streaming_step"


def _split_points():
    pts, acc = [], 0
    for s in _IN_SIZES[:-1]:
        acc += s
        pts.append(acc)
    return pts


def _rmsnorm(x, g):
    xf = x.astype(jnp.float32)
    y = xf * lax.rsqrt(jnp.mean(xf * xf, axis=-1, keepdims=True) + EPS)
    return (y * g.astype(jnp.float32)).astype(x.dtype)


def _chunk_gated_recurrence(q, k, v, log_g, s0):
    B, T, H, DK = q.shape
    DV = v.shape[-1]
    c = min(CHUNK, T)
    n = -(-T // c)
    pad = n * c - T

    def prep(a):
        a = a.astype(jnp.float32)
        a = jnp.pad(a, ((0, 0), (0, pad), (0, 0), (0, 0)))
        return a.reshape(B, n, c, H, a.shape[-1]).transpose(1, 0, 3, 2, 4)

    qs, ks, vs, gs = prep(q), prep(k), prep(v), prep(log_g)
    causal = jnp.tril(jnp.ones((c, c), dtype=bool))[:, :, None]

    def step(S, inp):
        qc, kc, vc, gc = inp
        b = jnp.cumsum(gc, axis=2)
        diff = b[:, :, :, None, :] - b[:, :, None, :, :]
        decay = jnp.exp(jnp.where(causal, diff, -jnp.inf))
        scores = jnp.einsum('bhtd,bhsd,bhtsd->bhts', qc, kc, decay)
        o = (jnp.einsum('bhts,bhsv->bhtv', scores, vc)
             + jnp.einsum('bhtd,bhdv->bhtv', qc * jnp.exp(b), S))
        b_last = b[:, :, -1:, :]
        S = (jnp.exp(b_last[:, :, 0, :])[..., None] * S
             + jnp.einsum('bhsd,bhsv->bhdv', kc * jnp.exp(b_last - b), vc))
        return S, o

    S, o = lax.scan(step, s0.astype(jnp.float32), (qs, ks, vs, gs))
    o = o.transpose(1, 0, 3, 2, 4).reshape(B, n * c, H, DV)[:, :T]
    return o, S


def _token_mixers(xn, w_in, lb, hgrn_norm, w_gla_a2, b_gla_a, gla_norm, w_out, st_h, st_g):
    B, T, _ = xn.shape
    f32 = jnp.float32
    z = xn @ w_in
    hq, hf, hi, hg, gq, gk, gv, gr, glr = jnp.split(z, _split_points(), axis=-1)

    hk = lambda a: a.reshape(B, T, HGRN_HEADS, HGRN_DK)
    hv = lambda a: a.reshape(B, T, HGRN_HEADS, HGRN_DV)
    lb_h = jnp.maximum(lb, 0.0).reshape(HGRN_HEADS, HGRN_DK)
    f_logit = hk(hf).astype(f32)
    log_f = jnp.logaddexp(jnp.log(lb_h), jnp.log1p(-lb_h) + jax.nn.log_sigmoid(f_logit))
    k_h = (1.0 - lb_h) * jax.nn.sigmoid(-f_logit)
    o_h, S_h = _chunk_gated_recurrence(jax.nn.silu(hk(hq)), k_h, hv(hi), log_f, st_h)
    o_h = _rmsnorm(o_h, hgrn_norm) * jax.nn.sigmoid(hv(hg).astype(f32))

    gdk = lambda a: a.reshape(B, T, GLA_HEADS, GLA_DK)
    gdv = lambda a: a.reshape(B, T, GLA_HEADS, GLA_DV)
    log_a = jax.nn.log_sigmoid((glr @ w_gla_a2 + b_gla_a).astype(f32)) / GLA_GATE_NORMALIZER
    o_g, S_g = _chunk_gated_recurrence(gdk(gq) * (GLA_DK ** -0.5), gdk(gk), gdv(gv), gdk(log_a), st_g)
    o_g = _rmsnorm(o_g, gla_norm) * jax.nn.silu(gdv(gr).astype(f32))

    o = jnp.concatenate([o_h.reshape(B, T, HGRN_WIDTH), o_g.reshape(B, T, GLA_WIDTH)],
                        axis=-1).astype(xn.dtype)
    return o @ w_out, S_h, S_g


def _peer(x, w_q, sub_keys, u, v):
    B, T, D = x.shape
    N = B * T
    xt = x.reshape(N, D)
    q = (xt @ w_q).reshape(N, PEER_HEADS, 2, PEER_DK // 2).astype(jnp.float32)
    s = jnp.einsum('nhcd,ckd->nhck', q, sub_keys.astype(jnp.float32))
    sv, si = lax.top_k(s, PEER_TOPK)
    cand = (sv[:, :, 0, :, None] + sv[:, :, 1, None, :]).reshape(N, PEER_HEADS, PEER_TOPK * PEER_TOPK)
    cidx = (si[:, :, 0, :, None] * N_KEYS + si[:, :, 1, None, :]).reshape(N, PEER_HEADS, PEER_TOPK * PEER_TOPK)
    fv, fi = lax.top_k(cand, PEER_TOPK)
    eidx = jnp.take_along_axis(cidx, fi, axis=-1).reshape(N, PEER_HEADS * PEER_TOPK)
    gates = jax.nn.softmax(fv, axis=-1).reshape(N, PEER_HEADS * PEER_TOPK).astype(x.dtype)

    blk = min(PEER_BLOCK, N)
    nb = -(-N // blk)
    pad = nb * blk - N
    xs = jnp.pad(xt, ((0, pad), (0, 0))).reshape(nb, blk, D)
    es = jnp.pad(eidx, ((0, pad), (0, 0))).reshape(nb, blk, PEER_HEADS * PEER_TOPK)
    gs = jnp.pad(gates, ((0, pad), (0, 0))).reshape(nb, blk, PEER_HEADS * PEER_TOPK)

    def block_fn(args):
        xb, eb, gb = args
        ub = jnp.take(u, eb, axis=0)
        act = jax.nn.gelu(jnp.einsum('nd,ned->ne', xb, ub), approximate=False)
        vb = jnp.take(v, eb, axis=0)
        return jnp.einsum('ne,ned->nd', gb * act, vb)

    y = lax.map(block_fn, (xs, es, gs))
    return y.reshape(nb * blk, D)[:N].reshape(B, T, D)


def _trunk(x, p, st_h, st_g, g_mix, w_in, hgrn_lb_logits, hgrn_norm, w_gla_a2, b_gla_a,
           gla_norm, w_out, g_ffn, w_peer_q, peer_sub_keys, peer_u, peer_v, g_ple,
           w_ple_gate, w_ple_proj, g_final):
    s = jax.nn.softmax(hgrn_lb_logits.astype(jnp.float32), axis=0)
    lb_all = jnp.cumsum(s, axis=0) - s[0]
    h = x
    new_h, new_g = [], []
    for i in range(DEPTH):
        mix, S_h, S_g = _token_mixers(_rmsnorm(h, g_mix[i]), w_in[i], lb_all[i], hgrn_norm[i],
                                      w_gla_a2[i], b_gla_a[i], gla_norm[i], w_out[i],
                                      st_h[i], st_g[i])
        h = h + mix
        h = h + _peer(_rmsnorm(h, g_ffn[i]), w_peer_q[i], peer_sub_keys[i], peer_u[i], peer_v[i])
        gate = jax.nn.sigmoid(_rmsnorm(h, g_ple[i]) @ w_ple_gate[i])
        h = h + gate * (p[i] @ w_ple_proj[i])
        new_h.append(S_h)
        new_g.append(S_g)
    return _rmsnorm(h, g_final), jnp.stack(new_h), jnp.stack(new_g)


def setup_inputs(seed: int = 0) -> dict:
    key = jax.random.key(seed)
    ks = jax.random.split(key, 24)
    nrm = lambda k, shape, scale: jax.random.normal(k, shape, jnp.float32) * scale
    gain = lambda k, shape: 1.0 + 0.05 * jax.random.normal(k, shape, jnp.float32)
    return {
        "x_prompt": nrm(ks[0], (BATCH, SEQ, D_MODEL), 1.0),
        "x_sample": nrm(ks[1], (DEC_BATCH, DEC_SEQ, D_MODEL), 1.0),
        "state_hgrn": nrm(ks[2], (DEPTH, DEC_BATCH, HGRN_HEADS, HGRN_DK, HGRN_DV), 0.5),
        "state_gla": nrm(ks[3], (DEPTH, DEC_BATCH, GLA_HEADS, GLA_DK, GLA_DV), 1.0),
        "p_prompt": nrm(ks[4], (DEPTH, BATCH, SEQ, PLE_DIM), 1.0),
        "p_sample": nrm(ks[5], (DEPTH, DEC_BATCH, DEC_SEQ, PLE_DIM), 1.0),
        "g_mix": gain(ks[6], (DEPTH, D_MODEL)),
        "w_in": nrm(ks[7], (DEPTH, D_MODEL, IN_COLS), D_MODEL ** -0.5),
        "hgrn_lb_logits": nrm(ks[8], (DEPTH, HGRN_WIDTH), 0.5),
        "hgrn_norm": gain(ks[9], (DEPTH, HGRN_HEADS, HGRN_DV)),
        "w_gla_a2": nrm(ks[10], (DEPTH, GLA_LOW_RANK, GLA_KEY_WIDTH), GLA_LOW_RANK ** -0.5),
        "b_gla_a": nrm(ks[11], (DEPTH, GLA_KEY_WIDTH), 0.1),
        "gla_norm": gain(ks[12], (DEPTH, GLA_HEADS, GLA_DV)),
        "w_out": nrm(ks[13], (DEPTH, MIX_WIDTH, D_MODEL), MIX_WIDTH ** -0.5),
        "g_ffn": gain(ks[14], (DEPTH, D_MODEL)),
        "w_peer_q": nrm(ks[15], (DEPTH, D_MODEL, PEER_HEADS * PEER_DK), D_MODEL ** -0.5),
        "peer_sub_keys": nrm(ks[16], (DEPTH, 2, N_KEYS, PEER_DK // 2), (PEER_DK // 2) ** -0.5),
        "peer_u": nrm(ks[17], (DEPTH, N_EXPERTS, D_MODEL), D_MODEL ** -0.5),
        "peer_v": nrm(ks[18], (DEPTH, N_EXPERTS, D_MODEL), PEER_HEADS ** -0.5),
        "g_ple": gain(ks[19], (DEPTH, D_MODEL)),
        "w_ple_gate": nrm(ks[20], (DEPTH, D_MODEL, D_MODEL), D_MODEL ** -0.5),
        "w_ple_proj": nrm(ks[21], (DEPTH, PLE_DIM, D_MODEL), PLE_DIM ** -0.5),
        "g_final": gain(ks[22], (D_MODEL,)),
    }


def reference(x_prompt, x_sample, state_hgrn, state_gla, p_prompt, p_sample, g_mix, w_in,
              hgrn_lb_logits, hgrn_norm, w_gla_a2, b_gla_a, gla_norm, w_out, g_ffn, w_peer_q,
              peer_sub_keys, peer_u, peer_v, g_ple, w_ple_gate, w_ple_proj, g_final):
    zeros_h = jnp.zeros((DEPTH, x_prompt.shape[0], HGRN_HEADS, HGRN_DK, HGRN_DV), jnp.float32)
    zeros_g = jnp.zeros((DEPTH, x_prompt.shape[0], GLA_HEADS, GLA_DK, GLA_DV), jnp.float32)
    y_prompt, new_state_hgrn_prompt, new_state_gla_prompt = _trunk(
        x_prompt, p_prompt, zeros_h, zeros_g, g_mix, w_in, hgrn_lb_logits, hgrn_norm, w_gla_a2,
        b_gla_a, gla_norm, w_out, g_ffn, w_peer_q, peer_sub_keys, peer_u, peer_v, g_ple,
        w_ple_gate, w_ple_proj, g_final)
    y_sample, new_state_hgrn_sample, new_state_gla_sample = _trunk(
        x_sample, p_sample, state_hgrn, state_gla, g_mix, w_in, hgrn_lb_logits, hgrn_norm,
        w_gla_a2, b_gla_a, gla_norm, w_out, g_ffn, w_peer_q, peer_sub_keys, peer_u, peer_v,
        g_ple, w_ple_gate, w_ple_proj, g_final)
    return (y_prompt, y_sample, new_state_hgrn_prompt, new_state_gla_prompt,
            new_state_hgrn_sample, new_state_gla_sample)
```

```python
import functools

import jax
import jax.numpy as jnp
from jax import lax
from jax.experimental import pallas as pl
from jax.experimental.pallas import tpu as pltpu

F32 = jnp.float32
BF16 = jnp.bfloat16
EPS = 1e-6
LANE = 128
SUB = 16
CHUNK = 64
TOPK = 16
GATE_NORMALIZER = 16.0
NEG_INF = float("-inf")
MIB = 1024 * 1024

_NT = (((1,), (1,)), ((), ()))
_TN = (((0,), (0,)), ((), ()))


def _params(sem, vmem_mib):
    return pltpu.CompilerParams(dimension_semantics=sem, vmem_limit_bytes=vmem_mib * MIB)


def _rms(x, g):
    return x * lax.rsqrt(jnp.mean(x * x, axis=-1, keepdims=True) + EPS) * g


def _log_sigmoid(x):
    return jnp.minimum(x, 0.0) - jnp.log1p(jnp.exp(-jnp.abs(x)))


def _tile(n, pref):
    return pref if n % pref == 0 else n


def _rows_loop(n_rows, fn, rows=128):
    rows = min(rows, n_rows)

    def body(i, carry):
        fn(pl.ds(pl.multiple_of(i * rows, rows), rows))
        return carry

    lax.fori_loop(0, n_rows // rows, body, 0)


def _in_proj_kernel(x_ref, g_ref, w_ref, wlr_ref, wa2_ref, ba_ref, z_ref, la_ref, xn_ref):
    @pl.when(pl.program_id(1) == 0)
    def _():
        def norm(rs):
            xn_ref[rs, :] = _rms(x_ref[rs, :], g_ref[...]).astype(BF16)

        _rows_loop(x_ref.shape[0], norm)
        glr = jnp.dot(xn_ref[...], wlr_ref[...], preferred_element_type=F32)
        ga = jnp.dot(glr.astype(BF16), wa2_ref[...], preferred_element_type=F32) + ba_ref[...]
        la = _log_sigmoid(ga) * (1.0 / GATE_NORMALIZER)
        for s in range(la_ref.shape[0]):
            la_ref[s] = la[:, s * LANE:(s + 1) * LANE]

    res = jnp.dot(xn_ref[...], w_ref[...], preferred_element_type=F32)
    for s in range(z_ref.shape[0]):
        z_ref[s] = res[:, s * LANE:(s + 1) * LANE]


def _in_proj(h, g, w_main, w_lr, w_a2, b_a):
    n, d = h.shape
    cols = w_main.shape[1]
    kw = w_a2.shape[1]
    tm = _tile(n, 512)
    tn = _tile(cols, 1024)
    return pl.pallas_call(
        _in_proj_kernel,
        grid=(n // tm, cols // tn),
        in_specs=[
            pl.BlockSpec((tm, d), lambda i, j: (i, 0)),
            pl.BlockSpec((1, d), lambda i, j: (0, 0)),
            pl.BlockSpec((d, tn), lambda i, j: (0, j)),
            pl.BlockSpec((d, LANE), lambda i, j: (0, 0)),
            pl.BlockSpec((LANE, kw), lambda i, j: (0, 0)),
            pl.BlockSpec((1, kw), lambda i, j: (0, 0)),
        ],
        out_specs=[
            pl.BlockSpec((tn // LANE, tm, LANE), lambda i, j: (j, i, 0)),
            pl.BlockSpec((kw // LANE, tm, LANE), lambda i, j: (0, i, 0)),
        ],
        out_shape=[
            jax.ShapeDtypeStruct((cols // LANE, n, LANE), F32),
            jax.ShapeDtypeStruct((kw // LANE, n, LANE), F32),
        ],
        scratch_shapes=[pltpu.VMEM((tm, d), BF16)],
        compiler_params=_params(("parallel", "arbitrary"), 52),
        name="in_proj",
    )(h, g, w_main, w_lr, w_a2, b_a)


def _chunk_step(q, k, v, g, st_ref, b_ref, k_ref, c):
    dk = q.shape[1]
    nsub = c // SUB
    row = lax.broadcasted_iota(jnp.int32, (c, c), 0)
    col = lax.broadcasted_iota(jnp.int32, (c, c), 1)
    tril = jnp.where(row >= col, 1.0, 0.0).astype(F32)
    b = jnp.dot(tril, g, precision=lax.Precision.HIGHEST, preferred_element_type=F32)
    b_ref[...] = b
    k_ref[...] = k
    vb = v.astype(BF16)
    st = st_ref[...]
    stb = st.astype(BF16)
    ones = jnp.ones((dk, c), BF16)
    row_k = lax.broadcasted_iota(jnp.int32, (c, dk), 0)
    row_t = lax.broadcasted_iota(jnp.int32, (SUB, dk), 0)
    lane_c = lax.broadcasted_iota(jnp.int32, (SUB, c), 1)
    outs = []
    for blk in range(nsub):
        lo = blk * SUB
        b_blk = b[lo:lo + SUB]
        q_blk = q[lo:lo + SUB]
        if blk > 0:
            ref = b_ref[pl.ds(lo - 1, 1), :]
            q_dec = q_blk * jnp.exp(b_blk - ref)
            k_dec = jnp.where(row_k < lo, k * jnp.exp(jnp.minimum(ref - b, 0.0)), 0.0)
            a = lax.dot_general(q_dec.astype(BF16), k_dec.astype(BF16), _NT,
                                preferred_element_type=F32)
        else:
            a = jnp.zeros((SUB, c), F32)
        prods = []
        for s in range(SUB):
            k_s = k_ref[pl.ds(lo + s, 1), :]
            b_s = b_ref[pl.ds(lo + s, 1), :]
            p = q_blk * k_s * jnp.exp(jnp.minimum(b_blk - b_s, 0.0))
            prods.append(jnp.where(row_t >= s, p, 0.0).astype(BF16))
        sums = jnp.dot(jnp.concatenate(prods, axis=0), ones, preferred_element_type=F32)
        for s in range(SUB):
            a = a + jnp.where(lane_c == lo + s, sums[s * SUB:(s + 1) * SUB], 0.0)
        q_abs = q_blk * jnp.exp(b_blk)
        outs.append(jnp.dot(a.astype(BF16), vb, preferred_element_type=F32)
                    + lax.dot_general(q_abs.astype(BF16), stb, _NT, preferred_element_type=F32))
    b_last = b_ref[pl.ds(c - 1, 1), :]
    k_end = (k * jnp.exp(b_last - b)).astype(BF16)
    st_ref[...] = st * jnp.exp(b_last) + lax.dot_general(vb, k_end, _TN,
                                                         preferred_element_type=F32)
    return jnp.concatenate(outs, axis=0)


def _slabs(ref, r0, c):
    return jnp.concatenate([ref[s, pl.ds(r0, c), :] for s in range(ref.shape[0])], axis=-1)


def _rec_kernel(*refs, mode, c, nchunk, has_state):
    if has_state:
        (q_ref, k_ref, v_ref, r_ref, aux_ref, gain_ref, s0_ref,
         o_ref, sout_ref, st_ref, b_sc, k_sc) = refs
    else:
        (q_ref, k_ref, v_ref, r_ref, aux_ref, gain_ref,
         o_ref, sout_ref, st_ref, b_sc, k_sc) = refs
        s0_ref = None
    t = pl.program_id(2)

    @pl.when(t == 0)
    def _():
        if has_state:
            st_ref[...] = s0_ref[0, 0].T
        else:
            st_ref[...] = jnp.zeros(st_ref.shape, F32)

    dk = st_ref.shape[1]

    def chunk(ci, carry):
        r0 = pl.multiple_of(ci * c, c)
        xq = _slabs(q_ref, r0, c)
        xk = _slabs(k_ref, r0, c)
        xv = _slabs(v_ref, r0, c)
        xr = _slabs(r_ref, r0, c)
        if mode == "hgrn":
            one_m_lb = aux_ref[0, 0:1, :]
            log_lb = aux_ref[0, 1:2, :]
            log_1m_lb = aux_ref[0, 2:3, :]
            q = xq * jax.nn.sigmoid(xq)
            k = one_m_lb * jax.nn.sigmoid(-xk)
            gated = log_1m_lb + _log_sigmoid(xk)
            g = jnp.maximum(log_lb, gated) + jnp.log1p(jnp.exp(-jnp.abs(log_lb - gated)))
            out_gate = jax.nn.sigmoid(xr)
        else:
            q = xq * (dk ** -0.5)
            k = xk
            g = _slabs(aux_ref, r0, c)
            out_gate = xr * jax.nn.sigmoid(xr)
        o = _chunk_step(q, k, xv, g, st_ref, b_sc, k_sc, c)
        o_ref[pl.ds(r0, c), :] = (_rms(o, gain_ref[0]) * out_gate).astype(o_ref.dtype)
        return carry

    lax.fori_loop(0, nchunk, chunk, 0)

    @pl.when(t == pl.num_programs(2) - 1)
    def _():
        sout_ref[0, 0] = st_ref[...].T


def _recurrence(z3, aux, gain, s0, *, mode, batch, seq, heads, dk, dv, q_off, k_off, v_off, r_off,
                aux_is_slab):
    c = min(CHUNK, seq)
    tb = _tile(seq, 256) if seq >= 256 else seq
    nchunk = tb // c
    nt = seq // tb
    kq, kv = dk // LANE, dv // LANE
    has_state = s0 is not None

    def slab_spec(width, off):
        return pl.BlockSpec((width, tb, LANE), lambda b, h, t: (off // width + h, b * nt + t, 0))

    in_specs = [slab_spec(kq, q_off), slab_spec(kq, k_off), slab_spec(kv, v_off),
                slab_spec(kv, r_off)]
    if aux_is_slab:
        in_specs.append(pl.BlockSpec((kq, tb, LANE), lambda b, h, t: (h, b * nt + t, 0)))
    else:
        in_specs.append(pl.BlockSpec((1, 8, dk), lambda b, h, t: (h, 0, 0)))
    in_specs.append(pl.BlockSpec((1, 1, dv), lambda b, h, t: (h, 0, 0)))
    args = [z3, z3, z3, z3, aux, gain]
    if has_state:
        in_specs.append(pl.BlockSpec((1, 1, dk, dv), lambda b, h, t: (b, h, 0, 0)))
        args.append(s0)
    kern = functools.partial(_rec_kernel, mode=mode, c=c, nchunk=nchunk, has_state=has_state)
    return pl.pallas_call(
        kern,
        grid=(batch, heads, nt),
        in_specs=in_specs,
        out_specs=[
            pl.BlockSpec((tb, dv), lambda b, h, t: (b * nt + t, h)),
            pl.BlockSpec((1, 1, dk, dv), lambda b, h, t: (b, h, 0, 0)),
        ],
        out_shape=[
            jax.ShapeDtypeStruct((batch * seq, heads * dv), BF16),
            jax.ShapeDtypeStruct((batch, heads, dk, dv), F32),
        ],
        scratch_shapes=[pltpu.VMEM((dv, dk), F32), pltpu.VMEM((c, dk), F32),
                        pltpu.VMEM((c, dk), F32)],
        compiler_params=_params(("parallel", "parallel", "arbitrary"), 32),
        name=mode + "_recurrence",
    )(*args)


def _out_proj_kernel(oh_ref, og_ref, w1_ref, w2_ref, h_ref, out_ref):
    acc = jnp.dot(oh_ref[...], w1_ref[...], preferred_element_type=F32)
    acc = acc + jnp.dot(og_ref[...], w2_ref[...], preferred_element_type=F32)
    out_ref[...] = h_ref[...] + acc


def _out_proj(o_h, o_g, w_top, w_bot, h):
    n, d = h.shape
    tm = _tile(n, 512)
    tn = _tile(d, 1024)
    return pl.pallas_call(
        _out_proj_kernel,
        grid=(n // tm, d // tn),
        in_specs=[
            pl.BlockSpec((tm, o_h.shape[1]), lambda i, j: (i, 0)),
            pl.BlockSpec((tm, o_g.shape[1]), lambda i, j: (i, 0)),
            pl.BlockSpec((w_top.shape[0], tn), lambda i, j: (0, j)),
            pl.BlockSpec((w_bot.shape[0], tn), lambda i, j: (0, j)),
            pl.BlockSpec((tm, tn), lambda i, j: (i, j)),
        ],
        out_specs=pl.BlockSpec((tm, tn), lambda i, j: (i, j)),
        out_shape=jax.ShapeDtypeStruct((n, d), F32),
        compiler_params=_params(("parallel", "arbitrary"), 40),
        name="out_proj",
    )(o_h, o_g, w_top, w_bot, h)


def _top_values(x):
    rows = lax.broadcasted_iota(jnp.int32, (TOPK, LANE), 0)
    vals = jnp.full((TOPK, LANE), NEG_INF, F32)
    cnts = jnp.zeros((TOPK, LANE), F32)
    for a in range(TOPK):
        m = jnp.max(x, axis=0, keepdims=True)
        eq = x == m
        n_eq = jnp.sum(jnp.where(eq, 1.0, 0.0), axis=0, keepdims=True)
        vals = jnp.where(rows == a, m, vals)
        cnts = jnp.where(rows == a, n_eq, cnts)
        x = jnp.where(eq, NEG_INF, x)
    return vals, cnts


def _peer_score_kernel(h_ref, g_ref, wq_ref, keys_ref, xn_ref, s1_ref, e1_ref, tr_ref, er_ref,
                       sc_ref):
    @pl.when(pl.program_id(1) == 0)
    def _():
        def norm(rs):
            xn_ref[rs, :] = _rms(h_ref[rs, :], g_ref[...]).astype(BF16)

        _rows_loop(h_ref.shape[0], norm)

    pq =jnp.dot(xn_ref[...], wq_ref[...], preferred_element_type=F32)
    for cset in range(2):
        qc = pq[:, cset * LANE:(cset + 1) * LANE].astype(BF16)
        sc_ref[cset] = lax.dot_general(keys_ref[cset], qc, _NT, preferred_element_type=F32)

    def lanes(ci, carry):
        l0 = pl.multiple_of(ci * LANE, LANE)
        s0 = sc_ref[0, :, pl.ds(l0, LANE)]
        s1 = sc_ref[1, :, pl.ds(l0, LANE)]
        v0, n0 = _top_values(s0)
        v1, n1 = _top_values(s1)
        cand = jnp.concatenate([v0[a:a + 1] + v1 for a in range(TOPK)], axis=0)
        mult = jnp.concatenate([n0[a:a + 1] * n1 for a in range(TOPK)], axis=0)
        top = v0[0:1] + v1[0:1]
        tau = jnp.full((1, LANE), NEG_INF, F32)
        seen = jnp.zeros((1, LANE), F32)
        x = cand
        for _ in range(TOPK):
            m = jnp.max(x, axis=0, keepdims=True)
            eq = x == m
            tau = jnp.where(seen < TOPK, m, tau)
            seen = seen + jnp.sum(jnp.where(eq, mult, 0.0), axis=0, keepdims=True)
            x = jnp.where(eq, NEG_INF, x)
        z = jnp.sum(jnp.where(cand >= tau, mult * jnp.exp(cand - top), 0.0), axis=0,
                    keepdims=True)
        s1_ref[0, :, pl.ds(l0, LANE)] = s1
        e1_ref[0, :, pl.ds(l0, LANE)] = jnp.exp(s1 - v1[0:1])
        tr_ref[0, :, pl.ds(l0, LANE)] = tau - s0
        er_ref[0, :, pl.ds(l0, LANE)] = jnp.exp(s0 - v0[0:1]) / z
        return carry

    lax.fori_loop(0, sc_ref.shape[2] // LANE, lanes, 0)


def _peer_scores(h, g, wq, keys):
    n, d = h.shape
    heads = wq.shape[1] // (2 * LANE)
    tm = _tile(n, 512)
    aux_spec = pl.BlockSpec((1, LANE, tm), lambda i, hd: (hd, 0, i))
    aux_shape = jax.ShapeDtypeStruct((heads, LANE, n), F32)
    return pl.pallas_call(
        _peer_score_kernel,
        grid=(n // tm, heads),
        in_specs=[
            pl.BlockSpec((tm, d), lambda i, hd: (i, 0)),
            pl.BlockSpec((1, d), lambda i, hd: (0, 0)),
            pl.BlockSpec((d, 2 * LANE), lambda i, hd: (0, hd)),
            pl.BlockSpec((2, LANE, LANE), lambda i, hd: (0, 0, 0)),
        ],
        out_specs=[pl.BlockSpec((tm, d), lambda i, hd: (i, 0)), aux_spec, aux_spec, aux_spec,
                   aux_spec],
        out_shape=[jax.ShapeDtypeStruct((n, d), BF16), aux_shape, aux_shape, aux_shape,
                   aux_shape],
        scratch_shapes=[pltpu.VMEM((2, LANE, tm), F32)],
        compiler_params=_params(("parallel", "arbitrary"), 40),
        name="peer_scores",
    )(h, g, wq, keys)


def _peer_dense_kernel(xn_ref, u_ref, v_ref, s1_ref, e1_ref, tr_ref, er_ref, y_ref):
    e = pl.program_id(1)
    heads = s1_ref.shape[0]
    tiles = u_ref.shape[0] // LANE

    @pl.when(e == 0)
    def _():
        y_ref[...] = jnp.zeros(y_ref.shape, F32)

    act = lax.dot_general(xn_ref[...], u_ref[...], _NT, preferred_element_type=F32)
    gated = []
    for ii in range(tiles):
        i = e * tiles + ii
        wt = jnp.zeros((LANE, xn_ref.shape[0]), F32)
        for hd in range(heads):
            thr = tr_ref[hd, pl.ds(i, 1), :]
            scale = er_ref[hd, pl.ds(i, 1), :]
            wt = wt + jnp.where(s1_ref[hd] >= thr, e1_ref[hd] * scale, 0.0)
        a = act[:, ii * LANE:(ii + 1) * LANE]
        gelu = 0.5 * a * (1.0 + lax.erf(a * 0.7071067811865476))
        gated.append((wt.T * gelu).astype(BF16))
    y_ref[...] += jnp.dot(jnp.concatenate(gated, axis=1), v_ref[...],
                          preferred_element_type=F32)


def _peer_dense(xn, u, v, s1, e1, tr, er):
    n, d = xn.shape
    n_exp = u.shape[0]
    heads = s1.shape[0]
    tn = _tile(n, 512)
    te = 256
    aux_spec = pl.BlockSpec((heads, LANE, tn), lambda t, e: (0, 0, t))
    return pl.pallas_call(
        _peer_dense_kernel,
        grid=(n // tn, n_exp // te),
        in_specs=[
            pl.BlockSpec((tn, d), lambda t, e: (t, 0)),
            pl.BlockSpec((te, d), lambda t, e: (e, 0)),
            pl.BlockSpec((te, d), lambda t, e: (e, 0)),
            aux_spec, aux_spec, aux_spec, aux_spec,
        ],
        out_specs=pl.BlockSpec((tn, d), lambda t, e: (t, 0)),
        out_shape=jax.ShapeDtypeStruct((n, d), F32),
        compiler_params=_params(("parallel", "arbitrary"), 56),
        name="peer_dense",
    )(xn, u, v, s1, e1, tr, er)


def _ple_kernel(h_ref, y_ref, g_ref, wg_ref, p_ref, wp_ref, out_ref, h2_ref, xn_ref):
    j = pl.program_id(1)
    tn = out_ref.shape[1]

    @pl.when(j == 0)
    def _():
        def norm(rs):
            h2 = h_ref[rs, :] + y_ref[rs, :]
            xn_ref[rs, :] = _rms(h2, g_ref[...]).astype(BF16)
            for s in range(h2_ref.shape[0]):
                h2_ref[s, rs, :] = h2[:, s * tn:(s + 1) * tn]

        _rows_loop(h_ref.shape[0], norm)

    gate = jax.nn.sigmoid(jnp.dot(xn_ref[...], wg_ref[...], preferred_element_type=F32))
    proj = jnp.dot(p_ref[...].astype(BF16), wp_ref[...], preferred_element_type=F32)
    out_ref[...] = h2_ref[j] + gate * proj


def _ple(h, y, g, wg, p, wp):
    n, d = h.shape
    tm = _tile(n, 256)
    tn = _tile(d, 512)
    return pl.pallas_call(
        _ple_kernel,
        grid=(n // tm, d // tn),
        in_specs=[
            pl.BlockSpec((tm, d), lambda i, j: (i, 0)),
            pl.BlockSpec((tm, d), lambda i, j: (i, 0)),
            pl.BlockSpec((1, d), lambda i, j: (0, 0)),
            pl.BlockSpec((d, tn), lambda i, j: (0, j)),
            pl.BlockSpec((tm, p.shape[1]), lambda i, j: (i, 0)),
            pl.BlockSpec((p.shape[1], tn), lambda i, j: (0, j)),
        ],
        out_specs=pl.BlockSpec((tm, tn), lambda i, j: (i, j)),
        out_shape=jax.ShapeDtypeStruct((n, d), F32),
        scratch_shapes=[pltpu.VMEM((d // tn, tm, tn), F32), pltpu.VMEM((tm, d), BF16)],
        compiler_params=_params(("parallel", "arbitrary"), 40),
        name="ple",
    )(h, y, g, wg, p, wp)


def _final_norm_kernel(x_ref, g_ref, o_ref):
    def norm(rs):
        o_ref[rs, :] = _rms(x_ref[rs, :], g_ref[...])

    _rows_loop(x_ref.shape[0], norm)


def _final_norm(h, g):
    n, d = h.shape
    tm = _tile(n, 512)
    return pl.pallas_call(
        _final_norm_kernel,
        grid=(n // tm,),
        in_specs=[pl.BlockSpec((tm, d), lambda i: (i, 0)), pl.BlockSpec((1, d), lambda i: (0, 0))],
        out_specs=pl.BlockSpec((tm, d), lambda i: (i, 0)),
        out_shape=jax.ShapeDtypeStruct((n, d), F32),
        compiler_params=_params(("parallel",), 40),
        name="final_norm",
    )(h, g)


def _prepare(g_mix, w_in, hgrn_lb_logits, hgrn_norm, w_gla_a2, b_gla_a, gla_norm, w_out, g_ffn,
             w_peer_q, peer_sub_keys, peer_u, peer_v, g_ple, w_ple_gate, w_ple_proj):
    depth = w_in.shape[0]
    hh, hdv = hgrn_norm.shape[1:]
    hw = hh * hdv
    low_rank = w_gla_a2.shape[1]
    main = w_in.shape[2] - low_rank
    soft = jax.nn.softmax(hgrn_lb_logits.astype(F32), axis=0)
    lb = jnp.maximum(jnp.cumsum(soft, axis=0) - soft[0], 0.0).reshape(depth, hh, 1, hw // hh)
    lb_rows = jnp.concatenate([1.0 - lb, jnp.log(lb), jnp.log1p(-lb),
                               jnp.zeros((depth, hh, 5, hw // hh), F32)], axis=2)
    layers = []
    for i in range(depth):
        layers.append(dict(
            g_mix=g_mix[i][None], g_ffn=g_ffn[i][None], g_ple=g_ple[i][None],
            w_main=w_in[i, :, :main].astype(BF16),
            w_lr=jnp.pad(w_in[i, :, main:], ((0, 0), (0, LANE - low_rank))).astype(BF16),
            w_a2=jnp.pad(w_gla_a2[i], ((0, LANE - low_rank), (0, 0))).astype(BF16),
            b_a=b_gla_a[i][None],
            lb_rows=lb_rows[i],
            hgrn_norm=hgrn_norm[i][:, None, :], gla_norm=gla_norm[i][:, None, :],
            w_out_top=w_out[i, :hw].astype(BF16), w_out_bot=w_out[i, hw:].astype(BF16),
            w_peer_q=w_peer_q[i].astype(BF16), keys=peer_sub_keys[i].astype(BF16),
            u=peer_u[i].astype(BF16), v=peer_v[i].astype(BF16),
            w_ple_gate=w_ple_gate[i].astype(BF16), w_ple_proj=w_ple_proj[i].astype(BF16),
        ))
    return layers


def _trunk(x, p, st_h, st_g, layers, g_final):
    batch, seq, d = x.shape
    n = batch * seq
    h = x.reshape(n, d)
    new_h, new_g = [], []
    for i, lw in enumerate(layers):
        hh, _, hdk = lw["hgrn_norm"].shape
        gh, _, gdv = lw["gla_norm"].shape
        hw = hh * hdk
        gkw = lw["w_a2"].shape[1]
        gdk = gkw // gh
        z3, la3 = _in_proj(h, lw["g_mix"], lw["w_main"], lw["w_lr"], lw["w_a2"], lw["b_a"])
        s_hw = hw // LANE
        o_h, s_h = _recurrence(
            z3, lw["lb_rows"], lw["hgrn_norm"], None if st_h is None else st_h[i],
            mode="hgrn", batch=batch, seq=seq, heads=hh, dk=hdk, dv=hdk,
            q_off=0, k_off=s_hw, v_off=2 * s_hw, r_off=3 * s_hw, aux_is_slab=False)
        g0 = 4 * s_hw
        s_kw = gkw // LANE
        s_vw = gh * gdv // LANE
        o_g, s_g = _recurrence(
            z3, la3, lw["gla_norm"], None if st_g is None else st_g[i],
            mode="gla", batch=batch, seq=seq, heads=gh, dk=gdk, dv=gdv,
            q_off=g0, k_off=g0 + s_kw, v_off=g0 + 2 * s_kw, r_off=g0 + 2 * s_kw + s_vw,
            aux_is_slab=True)
        h = _out_proj(o_h, o_g, lw["w_out_top"], lw["w_out_bot"], h)
        xn, s1, e1, tr, er = _peer_scores(h, lw["g_ffn"], lw["w_peer_q"], lw["keys"])
        y = _peer_dense(xn, lw["u"], lw["v"], s1, e1, tr, er)
        h = _ple(h, y, lw["g_ple"], lw["w_ple_gate"], p[i].reshape(n, -1), lw["w_ple_proj"])
        new_h.append(s_h)
        new_g.append(s_g)
    out = _final_norm(h, g_final[None])
    return out.reshape(batch, seq, d), jnp.stack(new_h), jnp.stack(new_g)


def kernel(x_prompt, x_sample, state_hgrn, state_gla, p_prompt, p_sample, g_mix, w_in,
           hgrn_lb_logits, hgrn_norm, w_gla_a2, b_gla_a, gla_norm, w_out, g_ffn, w_peer_q,
           peer_sub_keys, peer_u, peer_v, g_ple, w_ple_gate, w_ple_proj, g_final):
    layers = _prepare(g_mix, w_in, hgrn_lb_logits, hgrn_norm, w_gla_a2, b_gla_a, gla_norm, w_out,
                      g_ffn, w_peer_q, peer_sub_keys, peer_u, peer_v, g_ple, w_ple_gate,
                      w_ple_proj)
    y_p, h_p, g_p = _trunk(x_prompt, p_prompt, None, None, layers, g_final)
    y_s, h_s, g_s = _trunk(x_sample, p_sample, state_hgrn, state_gla, layers, g_final)
    return y_p, y_s, h_p, g_p, h_s, g_s
```

```python
import functools

import jax
import jax.numpy as jnp
from jax import lax
from jax.experimental import pallas as pl
from jax.experimental.pallas import tpu as pltpu

F32 = jnp.float32
BF16 = jnp.bfloat16
EPS = 1e-6
LANE = 128
SUB = 16
CHUNK = 64
TOPK = 16
GATE_NORMALIZER = 16.0
GROUP = 2
NEG_INF = float("-inf")
LOG2E = 1.4426950408889634
MIB = 1024 * 1024

_NT = (((1,), (1,)), ((), ()))
_TN = (((0,), (0,)), ((), ()))


def _params(sem, vmem_mib):
    return pltpu.CompilerParams(dimension_semantics=sem, vmem_limit_bytes=vmem_mib * MIB)


def _rms(x, g):
    return x * lax.rsqrt(jnp.mean(x * x, axis=-1, keepdims=True) + EPS) * g


def _log_sigmoid(x):
    return jnp.minimum(x, 0.0) - jnp.log1p(jnp.exp(-jnp.abs(x)))


def _tile(n, pref):
    return pref if n % pref == 0 else n


def _rows_loop(n_rows, fn, rows=128):
    rows = min(rows, n_rows)

    def body(i, carry):
        fn(pl.ds(pl.multiple_of(i * rows, rows), rows))
        return carry

    lax.fori_loop(0, n_rows // rows, body, 0)


def _in_proj_kernel(x_ref, g_ref, w_ref, wlr_ref, wa2_ref, ba_ref, z_ref, la_ref, xn_ref):
    @pl.when(pl.program_id(1) == 0)
    def _():
        def norm(rs):
            xn_ref[rs, :] = _rms(x_ref[rs, :], g_ref[...]).astype(BF16)

        _rows_loop(x_ref.shape[0], norm)
        glr = jnp.dot(xn_ref[...], wlr_ref[...], preferred_element_type=F32)
        ga = jnp.dot(glr.astype(BF16), wa2_ref[...], preferred_element_type=F32) + ba_ref[...]
        la = _log_sigmoid(ga) * (1.0 / GATE_NORMALIZER)
        for s in range(la_ref.shape[0]):
            la_ref[s] = la[:, s * LANE:(s + 1) * LANE]

    res = jnp.dot(xn_ref[...], w_ref[...], preferred_element_type=F32)
    for s in range(z_ref.shape[0]):
        z_ref[s] = res[:, s * LANE:(s + 1) * LANE]


def _in_proj(h, g, w_main, w_lr, w_a2, b_a):
    n, d = h.shape
    cols = w_main.shape[1]
    kw = w_a2.shape[1]
    tm = _tile(n, 512)
    tn = _tile(cols, 1024)
    return pl.pallas_call(
        _in_proj_kernel,
        grid=(n // tm, cols // tn),
        in_specs=[
            pl.BlockSpec((tm, d), lambda i, j: (i, 0)),
            pl.BlockSpec((1, d), lambda i, j: (0, 0)),
            pl.BlockSpec((d, tn), lambda i, j: (0, j)),
            pl.BlockSpec((d, LANE), lambda i, j: (0, 0)),
            pl.BlockSpec((LANE, kw), lambda i, j: (0, 0)),
            pl.BlockSpec((1, kw), lambda i, j: (0, 0)),
        ],
        out_specs=[
            pl.BlockSpec((tn // LANE, tm, LANE), lambda i, j: (j, i, 0)),
            pl.BlockSpec((kw // LANE, tm, LANE), lambda i, j: (0, i, 0)),
        ],
        out_shape=[
            jax.ShapeDtypeStruct((cols // LANE, n, LANE), F32),
            jax.ShapeDtypeStruct((kw // LANE, n, LANE), F32),
        ],
        scratch_shapes=[pltpu.VMEM((tm, d), BF16)],
        compiler_params=_params(("parallel", "arbitrary"), 52),
        name="in_proj",
    )(h, g, w_main, w_lr, w_a2, b_a)


def _chunk_step(q, k, v, g, st_ref, b_ref, k_ref, c):
    dk = q.shape[1]
    nsub = c // SUB
    row = lax.broadcasted_iota(jnp.int32, (c, c), 0)
    col = lax.broadcasted_iota(jnp.int32, (c, c), 1)
    tril = jnp.where(row >= col, 1.0, 0.0).astype(F32)
    b = jnp.dot(tril, g * LOG2E, precision=lax.Precision.HIGHEST, preferred_element_type=F32)
    b_ref[...] = b
    k_ref[...] = k
    vb = v.astype(BF16)
    st = st_ref[...]
    stb = st.astype(BF16)
    ones = jnp.ones((dk, c), BF16)
    a_rows = [jnp.zeros((SUB, c), F32)]
    for blk in range(1, nsub):
        lo = blk * SUB
        ref = b_ref[pl.ds(lo - 1, 1), :]
        q_dec = q[lo:lo + SUB] * jnp.exp2(b[lo:lo + SUB] - ref)
        k_dec = jnp.concatenate(
            [(k[:lo] * jnp.exp2(ref - b[:lo])).astype(BF16), jnp.zeros((c - lo, dk), BF16)],
            axis=0)
        a_rows.append(lax.dot_general(q_dec.astype(BF16), k_dec, _NT,
                                      preferred_element_type=F32))
    a = jnp.concatenate(a_rows, axis=0)
    prods = []
    for s in range(SUB):
        k_s = jnp.concatenate([jnp.broadcast_to(k_ref[pl.ds(blk * SUB + s, 1), :], (SUB, dk))
                               for blk in range(nsub)], axis=0)
        b_s = jnp.concatenate([jnp.broadcast_to(b_ref[pl.ds(blk * SUB + s, 1), :], (SUB, dk))
                               for blk in range(nsub)], axis=0)
        prods.append((q * k_s * jnp.exp2(b - b_s)).astype(BF16))
    sums = jnp.dot(jnp.concatenate(prods, axis=0), ones, preferred_element_type=F32)
    block_start = (row // SUB) * SUB
    for s in range(SUB):
        a = jnp.where(col == block_start + s, sums[s * c:(s + 1) * c], a)
    a = jnp.where(col <= row, a, 0.0)
    q_abs = q * jnp.exp2(b)
    o = (jnp.dot(a.astype(BF16), vb, preferred_element_type=F32)
         + lax.dot_general(q_abs.astype(BF16), stb, _NT, preferred_element_type=F32))
    b_last = b_ref[pl.ds(c - 1, 1), :]
    k_end = (k * jnp.exp2(b_last - b)).astype(BF16)
    st_ref[...] = st * jnp.exp2(b_last) + lax.dot_general(vb, k_end, _TN,
                                                          preferred_element_type=F32)
    return o


def _slabs(ref, first, count, r0, c):
    return jnp.concatenate([ref[first + s, pl.ds(r0, c), :] for s in range(count)], axis=-1)


def _rec_kernel(*refs, mode, c, nchunk, has_state):
    if has_state:
        (q_ref, k_ref, v_ref, r_ref, aux_ref, gain_ref, s0_ref,
         o_ref, sout_ref, st_ref, b_sc, k_sc) = refs
    else:
        (q_ref, k_ref, v_ref, r_ref, aux_ref, gain_ref,
         o_ref, sout_ref, st_ref, b_sc, k_sc) = refs
        s0_ref = None
    t = pl.program_id(2)
    hb, dv, dk = st_ref.shape
    kq, kv = dk // LANE, dv // LANE

    @pl.when(t == 0)
    def _():
        for hd in range(hb):
            if has_state:
                st_ref[hd] = s0_ref[0, hd].T
            else:
                st_ref[hd] = jnp.zeros((dv, dk), F32)

    def chunk(ci, carry):
        r0 = pl.multiple_of(ci * c, c)
        for hd in range(hb):
            xq = _slabs(q_ref, hd * kq, kq, r0, c)
            xk = _slabs(k_ref, hd * kq, kq, r0, c)
            xv = _slabs(v_ref, hd * kv, kv, r0, c)
            xr = _slabs(r_ref, hd * kv, kv, r0, c)
            if mode == "hgrn":
                one_m_lb = aux_ref[hd, 0:1, :]
                log_lb = aux_ref[hd, 1:2, :]
                log_1m_lb = aux_ref[hd, 2:3, :]
                q = xq * jax.nn.sigmoid(xq)
                k = one_m_lb * jax.nn.sigmoid(-xk)
                gated = log_1m_lb + _log_sigmoid(xk)
                g = jnp.maximum(log_lb, gated) + jnp.log1p(jnp.exp(-jnp.abs(log_lb - gated)))
                out_gate = jax.nn.sigmoid(xr)
            else:
                q = xq * (dk ** -0.5)
                k = xk
                g = _slabs(aux_ref, hd * kq, kq, r0, c)
                out_gate = xr * jax.nn.sigmoid(xr)
            o = _chunk_step(q, k, xv, g, st_ref.at[hd], b_sc.at[hd], k_sc.at[hd], c)
            o_ref[pl.ds(r0, c), hd * dv:(hd + 1) * dv] = (
                _rms(o, gain_ref[hd]) * out_gate).astype(o_ref.dtype)
        return carry

    lax.fori_loop(0, nchunk, chunk, 0)

    @pl.when(t == pl.num_programs(2) - 1)
    def _():
        for hd in range(hb):
            sout_ref[0, hd] = st_ref[hd].T


def _recurrence(z3, aux, gain, s0, *, mode, batch, seq, heads, dk, dv, q_off, k_off, v_off, r_off,
                aux_is_slab, heads_per_step):
    c = min(CHUNK, seq)
    tb = _tile(seq, 256) if seq >= 256 else seq
    nchunk = tb // c
    nt = seq // tb
    kq, kv = dk // LANE, dv // LANE
    has_state = s0 is not None
    hb = min(heads, heads_per_step)
    assert heads % hb == 0

    def slab_spec(per_head, off):
        width = per_head * hb
        assert off % width == 0
        return pl.BlockSpec((width, tb, LANE), lambda b, h, t: (off // width + h, b * nt + t, 0))

    in_specs = [slab_spec(kq, q_off), slab_spec(kq, k_off), slab_spec(kv, v_off),
                slab_spec(kv, r_off)]
    if aux_is_slab:
        in_specs.append(slab_spec(kq, 0))
    else:
        in_specs.append(pl.BlockSpec((hb, 8, dk), lambda b, h, t: (h, 0, 0)))
    in_specs.append(pl.BlockSpec((hb, 1, dv), lambda b, h, t: (h, 0, 0)))
    args = [z3, z3, z3, z3, aux, gain]
    if has_state:
        in_specs.append(pl.BlockSpec((1, hb, dk, dv), lambda b, h, t: (b, h, 0, 0)))
        args.append(s0)
    kern = functools.partial(_rec_kernel, mode=mode, c=c, nchunk=nchunk, has_state=has_state)
    return pl.pallas_call(
        kern,
        grid=(batch, heads // hb, nt),
        in_specs=in_specs,
        out_specs=[
            pl.BlockSpec((tb, hb * dv), lambda b, h, t: (b * nt + t, h)),
            pl.BlockSpec((1, hb, dk, dv), lambda b, h, t: (b, h, 0, 0)),
        ],
        out_shape=[
            jax.ShapeDtypeStruct((batch * seq, heads * dv), BF16),
            jax.ShapeDtypeStruct((batch, heads, dk, dv), F32),
        ],
        scratch_shapes=[pltpu.VMEM((hb, dv, dk), F32), pltpu.VMEM((hb, c, dk), F32),
                        pltpu.VMEM((hb, c, dk), F32)],
        compiler_params=_params(("parallel", "parallel", "arbitrary"), 40),
        name=mode + "_recurrence",
    )(*args)


def _out_proj_kernel(oh_ref, og_ref, w1_ref, w2_ref, h_ref, out_ref):
    acc = jnp.dot(oh_ref[...], w1_ref[...], preferred_element_type=F32)
    acc = acc + jnp.dot(og_ref[...], w2_ref[...], preferred_element_type=F32)
    out_ref[...] = h_ref[...] + acc


def _out_proj(o_h, o_g, w_top, w_bot, h):
    n, d = h.shape
    tm = _tile(n, 512)
    tn = _tile(d, 1024)
    return pl.pallas_call(
        _out_proj_kernel,
        grid=(n // tm, d // tn),
        in_specs=[
            pl.BlockSpec((tm, o_h.shape[1]), lambda i, j: (i, 0)),
            pl.BlockSpec((tm, o_g.shape[1]), lambda i, j: (i, 0)),
            pl.BlockSpec((w_top.shape[0], tn), lambda i, j: (0, j)),
            pl.BlockSpec((w_bot.shape[0], tn), lambda i, j: (0, j)),
            pl.BlockSpec((tm, tn), lambda i, j: (i, j)),
        ],
        out_specs=pl.BlockSpec((tm, tn), lambda i, j: (i, j)),
        out_shape=jax.ShapeDtypeStruct((n, d), F32),
        compiler_params=_params(("parallel", "arbitrary"), 40),
        name="out_proj",
    )(o_h, o_g, w_top, w_bot, h)


def _top_values(x):
    rows = lax.broadcasted_iota(jnp.int32, (TOPK, LANE), 0)
    vals = jnp.full((TOPK, LANE), NEG_INF, F32)
    cnts = jnp.zeros((TOPK, LANE), F32)
    for a in range(TOPK):
        m = jnp.max(x, axis=0, keepdims=True)
        eq = x == m
        n_eq = jnp.sum(jnp.where(eq, 1.0, 0.0), axis=0, keepdims=True)
        vals = jnp.where(rows == a, m, vals)
        cnts = jnp.where(rows == a, n_eq, cnts)
        x = jnp.where(eq, NEG_INF, x)
    return vals, cnts


def _candidate_sums(v0, n0, v1, n1):
    half = TOPK // 2
    r16 = lax.broadcasted_iota(jnp.int32, (TOPK, LANE), 0)
    r8 = lax.broadcasted_iota(jnp.int32, (half, LANE), 0)
    pieces = [
        (v0[0:1] + v1, n0[0:1] * n1, None),
        (v0[1:2] + v1[:half], n0[1:2] * n1[:half], None),
        (v0[2:3] + v1[:half], n0[2:3] * n1[:half], r8 < 5),
        (v0[3:4] + v1[:half], n0[3:4] * n1[:half], r8 < 4),
        (v0 + v1[0:1], n0 * n1[0:1], r16 >= 4),
        (v0[:half] + v1[1:2], n0[:half] * n1[1:2], r8 >= 4),
        (v0[:half] + v1[2:3], n0[:half] * n1[2:3], r8 == 4),
    ]
    cand = jnp.concatenate([c if m is None else jnp.where(m, c, NEG_INF) for c, _, m in pieces],
                           axis=0)
    mult = jnp.concatenate([n if m is None else jnp.where(m, n, 0.0) for _, n, m in pieces],
                           axis=0)
    return cand, mult


def _peer_score_kernel(h_ref, g_ref, wq_ref, keys_ref, xn_ref, s1_ref, e1_ref, tr_ref, er_ref,
                       sc_ref):
    @pl.when(pl.program_id(1) == 0)
    def _():
        def norm(rs):
            xn_ref[rs, :] = _rms(h_ref[rs, :], g_ref[...]).astype(BF16)

        _rows_loop(h_ref.shape[0], norm)

    pq =jnp.dot(xn_ref[...], wq_ref[...], preferred_element_type=F32)
    for cset in range(2):
        qc = pq[:, cset * LANE:(cset + 1) * LANE].astype(BF16)
        sc_ref[cset] = lax.dot_general(keys_ref[cset], qc, _NT, preferred_element_type=F32)

    def lanes(ci, carry):
        l0 = pl.multiple_of(ci * LANE, LANE)
        s0 = sc_ref[0, :, pl.ds(l0, LANE)]
        s1 = sc_ref[1, :, pl.ds(l0, LANE)]
        v0, n0 = _top_values(s0)
        v1, n1 = _top_values(s1)
        cand, mult = _candidate_sums(v0, n0, v1, n1)
        top = v0[0:1] + v1[0:1]
        tau = jnp.full((1, LANE), NEG_INF, F32)
        seen = jnp.zeros((1, LANE), F32)
        x = cand
        for _ in range(TOPK):
            m = jnp.max(x, axis=0, keepdims=True)
            eq = x == m
            tau = jnp.where(seen < TOPK, m, tau)
            seen = seen + jnp.sum(jnp.where(eq, mult, 0.0), axis=0, keepdims=True)
            x = jnp.where(eq, NEG_INF, x)
        z = jnp.sum(jnp.where(cand >= tau, mult * jnp.exp(cand - top), 0.0), axis=0,
                    keepdims=True)
        s1_ref[0, :, pl.ds(l0, LANE)] = s1
        e1_ref[0, :, pl.ds(l0, LANE)] = jnp.exp(s1 - v1[0:1])
        tr_ref[0, :, pl.ds(l0, LANE)] = tau - s0
        er_ref[0, :, pl.ds(l0, LANE)] = jnp.exp(s0 - v0[0:1]) / z
        return carry

    lax.fori_loop(0, sc_ref.shape[2] // LANE, lanes, 0)


def _peer_scores(h, g, wq, keys):
    n, d = h.shape
    heads = wq.shape[1] // (2 * LANE)
    tm = _tile(n, 512)
    aux_spec = pl.BlockSpec((1, LANE, tm), lambda i, hd: (hd, 0, i))
    aux_shape = jax.ShapeDtypeStruct((heads, LANE, n), F32)
    return pl.pallas_call(
        _peer_score_kernel,
        grid=(n // tm, heads),
        in_specs=[
            pl.BlockSpec((tm, d), lambda i, hd: (i, 0)),
            pl.BlockSpec((1, d), lambda i, hd: (0, 0)),
            pl.BlockSpec((d, 2 * LANE), lambda i, hd: (0, hd)),
            pl.BlockSpec((2, LANE, LANE), lambda i, hd: (0, 0, 0)),
        ],
        out_specs=[pl.BlockSpec((tm, d), lambda i, hd: (i, 0)), aux_spec, aux_spec, aux_spec,
                   aux_spec],
        out_shape=[jax.ShapeDtypeStruct((n, d), BF16), aux_shape, aux_shape, aux_shape,
                   aux_shape],
        scratch_shapes=[pltpu.VMEM((2, LANE, tm), F32)],
        compiler_params=_params(("parallel", "arbitrary"), 40),
        name="peer_scores",
    )(h, g, wq, keys)


def _peer_dense_kernel(xn_ref, u_ref, v_ref, s1_ref, e1_ref, tr_ref, er_ref, y_ref):
    e = pl.program_id(1)
    heads = s1_ref.shape[0]
    tiles = u_ref.shape[0] // LANE

    @pl.when(e == 0)
    def _():
        y_ref[...] = jnp.zeros(y_ref.shape, F32)

    xn = xn_ref[...]
    gated = []
    for grp in range(tiles // GROUP):
        rows = slice(grp * GROUP * LANE, (grp + 1) * GROUP * LANE)
        act = lax.dot_general(xn, u_ref[rows, :], _NT, preferred_element_type=F32)
        for ii in range(GROUP):
            i = e * tiles + grp * GROUP + ii
            wt = jnp.zeros((LANE, xn_ref.shape[0]), F32)
            for hd in range(heads):
                thr = tr_ref[hd, pl.ds(i, 1), :]
                scale = er_ref[hd, pl.ds(i, 1), :]
                wt = wt + jnp.where(s1_ref[hd] >= thr, e1_ref[hd] * scale, 0.0)
            a = act[:, ii * LANE:(ii + 1) * LANE]
            gelu = 0.5 * a * (1.0 + lax.erf(a * 0.7071067811865476))
            gated.append((wt.T * gelu).astype(BF16))
    y_ref[...] += jnp.dot(jnp.concatenate(gated, axis=1), v_ref[...],
                          preferred_element_type=F32)


def _peer_dense(xn, u, v, s1, e1, tr, er):
    n, d = xn.shape
    n_exp = u.shape[0]
    heads = s1.shape[0]
    tn = _tile(n, 512)
    te = 512
    once = pl.Buffered(1)
    aux_spec = pl.BlockSpec((heads, LANE, tn), lambda t, e: (0, 0, t), pipeline_mode=once)
    return pl.pallas_call(
        _peer_dense_kernel,
        grid=(n // tn, n_exp // te),
        in_specs=[
            pl.BlockSpec((tn, d), lambda t, e: (t, 0), pipeline_mode=once),
            pl.BlockSpec((te, d), lambda t, e: (e, 0)),
            pl.BlockSpec((te, d), lambda t, e: (e, 0)),
            aux_spec, aux_spec, aux_spec, aux_spec,
        ],
        out_specs=pl.BlockSpec((tn, d), lambda t, e: (t, 0)),
        out_shape=jax.ShapeDtypeStruct((n, d), F32),
        compiler_params=_params(("parallel", "arbitrary"), 56),
        name="peer_dense",
    )(xn, u, v, s1, e1, tr, er)


def _add_norm_kernel(h_ref, y_ref, g_ref, h2_ref, xn_ref):
    def norm(rs):
        h2 = h_ref[rs, :] + y_ref[rs, :]
        h2_ref[rs, :] = h2
        xn_ref[rs, :] = _rms(h2, g_ref[...]).astype(BF16)

    _rows_loop(h_ref.shape[0], norm)


def _add_norm(h, y, g):
    n, d = h.shape
    tm = _tile(n, 256)
    row = pl.BlockSpec((tm, d), lambda i: (i, 0))
    return pl.pallas_call(
        _add_norm_kernel,
        grid=(n // tm,),
        in_specs=[row, row, pl.BlockSpec((1, d), lambda i: (0, 0))],
        out_specs=[row, row],
        out_shape=[jax.ShapeDtypeStruct((n, d), F32), jax.ShapeDtypeStruct((n, d), BF16)],
        compiler_params=_params(("parallel",), 40),
        name="add_norm",
    )(h, y, g)


def _ple_kernel(xn_ref, wg_ref, h2_ref, p_ref, wp_ref, out_ref):
    gate = jax.nn.sigmoid(jnp.dot(xn_ref[...], wg_ref[...], preferred_element_type=F32))
    proj = jnp.dot(p_ref[...].astype(BF16), wp_ref[...], preferred_element_type=F32)
    out_ref[...] = h2_ref[...] + gate * proj


def _ple(xn, h2, wg, p, wp):
    n, d = h2.shape
    tm = _tile(n, 1024)
    tn = _tile(d, 512)
    return pl.pallas_call(
        _ple_kernel,
        grid=(n // tm, d // tn),
        in_specs=[
            pl.BlockSpec((tm, d), lambda i, j: (i, 0)),
            pl.BlockSpec((d, tn), lambda i, j: (0, j)),
            pl.BlockSpec((tm, tn), lambda i, j: (i, j)),
            pl.BlockSpec((tm, p.shape[1]), lambda i, j: (i, 0)),
            pl.BlockSpec((p.shape[1], tn), lambda i, j: (0, j)),
        ],
        out_specs=pl.BlockSpec((tm, tn), lambda i, j: (i, j)),
        out_shape=jax.ShapeDtypeStruct((n, d), F32),
        compiler_params=_params(("parallel", "arbitrary"), 40),
        name="ple",
    )(xn, wg, h2, p, wp)


def _final_norm_kernel(x_ref, g_ref, o_ref):
    def norm(rs):
        o_ref[rs, :] = _rms(x_ref[rs, :], g_ref[...])

    _rows_loop(x_ref.shape[0], norm)


def _final_norm(h, g):
    n, d = h.shape
    tm = _tile(n, 512)
    return pl.pallas_call(
        _final_norm_kernel,
        grid=(n // tm,),
        in_specs=[pl.BlockSpec((tm, d), lambda i: (i, 0)), pl.BlockSpec((1, d), lambda i: (0, 0))],
        out_specs=pl.BlockSpec((tm, d), lambda i: (i, 0)),
        out_shape=jax.ShapeDtypeStruct((n, d), F32),
        compiler_params=_params(("parallel",), 40),
        name="final_norm",
    )(h, g)


def _prepare(g_mix, w_in, hgrn_lb_logits, hgrn_norm, w_gla_a2, b_gla_a, gla_norm, w_out, g_ffn,
             w_peer_q, peer_sub_keys, peer_u, peer_v, g_ple, w_ple_gate, w_ple_proj):
    depth = w_in.shape[0]
    hh, hdv = hgrn_norm.shape[1:]
    hw = hh * hdv
    low_rank = w_gla_a2.shape[1]
    main = w_in.shape[2] - low_rank
    soft = jax.nn.softmax(hgrn_lb_logits.astype(F32), axis=0)
    lb = jnp.maximum(jnp.cumsum(soft, axis=0) - soft[0], 0.0).reshape(depth, hh, 1, hw // hh)
    lb_rows = jnp.concatenate([1.0 - lb, jnp.log(lb), jnp.log1p(-lb),
                               jnp.zeros((depth, hh, 5, hw // hh), F32)], axis=2)
    layers = []
    for i in range(depth):
        layers.append(dict(
            g_mix=g_mix[i][None], g_ffn=g_ffn[i][None], g_ple=g_ple[i][None],
            w_main=w_in[i, :, :main].astype(BF16),
            w_lr=jnp.pad(w_in[i, :, main:], ((0, 0), (0, LANE - low_rank))).astype(BF16),
            w_a2=jnp.pad(w_gla_a2[i], ((0, LANE - low_rank), (0, 0))).astype(BF16),
            b_a=b_gla_a[i][None],
            lb_rows=lb_rows[i],
            hgrn_norm=hgrn_norm[i][:, None, :], gla_norm=gla_norm[i][:, None, :],
            w_out_top=w_out[i, :hw].astype(BF16), w_out_bot=w_out[i, hw:].astype(BF16),
            w_peer_q=w_peer_q[i].astype(BF16), keys=peer_sub_keys[i].astype(BF16),
            u=peer_u[i].astype(BF16), v=peer_v[i].astype(BF16),
            w_ple_gate=w_ple_gate[i].astype(BF16), w_ple_proj=w_ple_proj[i].astype(BF16),
        ))
    return layers


def _trunk(x, p, st_h, st_g, layers, g_final):
    batch, seq, d = x.shape
    n = batch * seq
    h = x.reshape(n, d)
    new_h, new_g = [], []
    for i, lw in enumerate(layers):
        hh, _, hdk = lw["hgrn_norm"].shape
        gh, _, gdv = lw["gla_norm"].shape
        hw = hh * hdk
        gkw = lw["w_a2"].shape[1]
        gdk = gkw // gh
        z3, la3 = _in_proj(h, lw["g_mix"], lw["w_main"], lw["w_lr"], lw["w_a2"], lw["b_a"])
        s_hw = hw // LANE
        o_h, s_h = _recurrence(
            z3, lw["lb_rows"], lw["hgrn_norm"], None if st_h is None else st_h[i],
            mode="hgrn", batch=batch, seq=seq, heads=hh, dk=hdk, dv=hdk,
            q_off=0, k_off=s_hw, v_off=2 * s_hw, r_off=3 * s_hw, aux_is_slab=False,
            heads_per_step=4)
        g0 = 4 * s_hw
        s_kw = gkw // LANE
        s_vw = gh * gdv // LANE
        o_g, s_g = _recurrence(
            z3, la3, lw["gla_norm"], None if st_g is None else st_g[i],
            mode="gla", batch=batch, seq=seq, heads=gh, dk=gdk, dv=gdv,
            q_off=g0, k_off=g0 + s_kw, v_off=g0 + 2 * s_kw, r_off=g0 + 2 * s_kw + s_vw,
            aux_is_slab=True, heads_per_step=2)
        h = _out_proj(o_h, o_g, lw["w_out_top"], lw["w_out_bot"], h)
        xn, s1, e1, tr, er = _peer_scores(h, lw["g_ffn"], lw["w_peer_q"], lw["keys"])
        y = _peer_dense(xn, lw["u"], lw["v"], s1, e1, tr, er)
        h2, xn = _add_norm(h, y, lw["g_ple"])
        h = _ple(xn, h2, lw["w_ple_gate"], p[i].reshape(n, -1), lw["w_ple_proj"])
        new_h.append(s_h)
        new_g.append(s_g)
    out = _final_norm(h, g_final[None])
    return out.reshape(batch, seq, d), jnp.stack(new_h), jnp.stack(new_g)


def kernel(x_prompt, x_sample, state_hgrn, state_gla, p_prompt, p_sample, g_mix, w_in,
           hgrn_lb_logits, hgrn_norm, w_gla_a2, b_gla_a, gla_norm, w_out, g_ffn, w_peer_q,
           peer_sub_keys, peer_u, peer_v, g_ple, w_ple_gate, w_ple_proj, g_final):
    layers = _prepare(g_mix, w_in, hgrn_lb_logits, hgrn_norm, w_gla_a2, b_gla_a, gla_norm, w_out,
                      g_ffn, w_peer_q, peer_sub_keys, peer_u, peer_v, g_ple, w_ple_gate,
                      w_ple_proj)
    y_p, h_p, g_p = _trunk(x_prompt, p_prompt, None, None, layers, g_final)
    y_s, h_s, g_s = _trunk(x_sample, p_sample, state_hgrn, state_gla, layers, g_final)
    return y_p, y_s, h_p, g_p, h_s, g_s
```

```python
import functools

import jax
import jax.numpy as jnp
from jax import lax
from jax.experimental import pallas as pl
from jax.experimental.pallas import tpu as pltpu

F32 = jnp.float32
BF16 = jnp.bfloat16
EPS = 1e-6
LANE = 128
SUB = 16
CHUNK = 64
TOPK = 16
GATE_NORMALIZER = 16.0
GROUP = 2
NEG_INF = float("-inf")
LOG2E = 1.4426950408889634
MIB = 1024 * 1024

_NT = (((1,), (1,)), ((), ()))
_TN = (((0,), (0,)), ((), ()))


def _params(sem, vmem_mib):
    return pltpu.CompilerParams(dimension_semantics=sem, vmem_limit_bytes=vmem_mib * MIB)


def _rms(x, g):
    return x * lax.rsqrt(jnp.mean(x * x, axis=-1, keepdims=True) + EPS) * g


def _log1p_exp_neg(x):
    return jnp.log(1.0 + jnp.exp(-x))


def _log_sigmoid(x):
    return jnp.minimum(x, 0.0) - _log1p_exp_neg(jnp.abs(x))


def _tile(n, pref):
    return pref if n % pref == 0 else n


def _rows_loop(n_rows, fn, rows=128):
    rows = min(rows, n_rows)

    def body(i, carry):
        fn(pl.ds(pl.multiple_of(i * rows, rows), rows))
        return carry

    lax.fori_loop(0, n_rows // rows, body, 0)


def _in_proj_kernel(x_ref, g_ref, w_ref, wlr_ref, wa2_ref, ba_ref, z_ref, la_ref, xn_ref):
    @pl.when(pl.program_id(1) == 0)
    def _():
        def norm(rs):
            xn_ref[rs, :] = _rms(x_ref[rs, :], g_ref[...]).astype(BF16)

        _rows_loop(x_ref.shape[0], norm)
        glr = jnp.dot(xn_ref[...], wlr_ref[...], preferred_element_type=F32)
        ga = jnp.dot(glr.astype(BF16), wa2_ref[...], preferred_element_type=F32) + ba_ref[...]
        la = _log_sigmoid(ga) * (1.0 / GATE_NORMALIZER)
        for s in range(la_ref.shape[0]):
            la_ref[s] = la[:, s * LANE:(s + 1) * LANE]

    res = jnp.dot(xn_ref[...], w_ref[...], preferred_element_type=F32)
    for s in range(z_ref.shape[0]):
        z_ref[s] = res[:, s * LANE:(s + 1) * LANE]


def _in_proj(h, g, w_all, layer, cols, w_lr, w_a2, b_a):
    n, d = h.shape
    kw = w_a2.shape[1]
    tm = _tile(n, 512)
    tn = _tile(cols, 1024)
    return pl.pallas_call(
        _in_proj_kernel,
        grid=(n // tm, cols // tn),
        in_specs=[
            pl.BlockSpec((tm, d), lambda i, j: (i, 0)),
            pl.BlockSpec((1, d), lambda i, j: (0, 0)),
            pl.BlockSpec((None, d, tn), lambda i, j: (layer, 0, j)),
            pl.BlockSpec((d, LANE), lambda i, j: (0, 0)),
            pl.BlockSpec((LANE, kw), lambda i, j: (0, 0)),
            pl.BlockSpec((1, kw), lambda i, j: (0, 0)),
        ],
        out_specs=[
            pl.BlockSpec((tn // LANE, tm, LANE), lambda i, j: (j, i, 0)),
            pl.BlockSpec((kw // LANE, tm, LANE), lambda i, j: (0, i, 0)),
        ],
        out_shape=[
            jax.ShapeDtypeStruct((cols // LANE, n, LANE), F32),
            jax.ShapeDtypeStruct((kw // LANE, n, LANE), F32),
        ],
        scratch_shapes=[pltpu.VMEM((tm, d), BF16)],
        compiler_params=_params(("parallel", "arbitrary"), 52),
        name="in_proj",
    )(h, g, w_all, w_lr, w_a2, b_a)


def _head_lanes(x, hd, hb):
    w = x.shape[1] // hb
    zero = jnp.zeros((x.shape[0], w), x.dtype)
    return jnp.concatenate([x[:, j * w:(j + 1) * w] if j == hd else zero for j in range(hb)],
                           axis=1)


def _rows_per_block(ref, s, nsub):
    return jnp.concatenate([jnp.broadcast_to(ref[pl.ds(j * SUB + s, 1), :], (SUB, ref.shape[1]))
                            for j in range(nsub)], axis=0)


def _chunk_step(q, k, v, g, ones_bd, st_ref, b_ref, k_ref, c, log2k=None):
    hb, dv, dk = st_ref.shape
    nsub = c // SUB
    row = lax.broadcasted_iota(jnp.int32, (c, c), 0)
    col = lax.broadcasted_iota(jnp.int32, (c, c), 1)
    tril = jnp.where(row >= col, 1.0, 0.0).astype(F32)
    b = jnp.dot(tril, g * LOG2E, precision=lax.Precision.HIGHEST, preferred_element_type=F32)
    b_ref[...] = b
    k_ref[...] = k if log2k is None else log2k - b
    vb = v.astype(BF16)
    b_last = b_ref[pl.ds(c - 1, 1), :]
    a_rows = [jnp.zeros((SUB, hb * c), F32)]
    for blk in range(1, nsub):
        lo = blk * SUB
        ref = b_ref[pl.ds(lo - 1, 1), :]
        q_dec = (q[lo:lo + SUB] * jnp.exp2(b[lo:lo + SUB] - ref)).astype(BF16)
        k_dec = jnp.concatenate(
            [(k[:lo] * jnp.exp2(ref - b[:lo])).astype(BF16), jnp.zeros((c - lo, hb * dk), BF16)],
            axis=0)
        k_bd = jnp.concatenate([_head_lanes(k_dec, hd, hb) for hd in range(hb)], axis=0)
        a_rows.append(lax.dot_general(q_dec, k_bd, _NT, preferred_element_type=F32))
    a = jnp.concatenate(a_rows, axis=0)
    prods = []
    for s in range(SUB):
        if log2k is None:
            k_s = _rows_per_block(k_ref, s, nsub)
            prods.append((q * k_s * jnp.exp2(b - _rows_per_block(b_ref, s, nsub))).astype(BF16))
        else:
            prods.append((q * jnp.exp2(b + _rows_per_block(k_ref, s, nsub))).astype(BF16))
    sums = jnp.dot(jnp.concatenate(prods, axis=0), ones_bd, preferred_element_type=F32)
    row_a = lax.broadcasted_iota(jnp.int32, (c, hb * c), 0)
    col_a = lax.broadcasted_iota(jnp.int32, (c, hb * c), 1) % c
    block_start = (row_a // SUB) * SUB
    for s in range(SUB):
        a = jnp.where(col_a == block_start + s, sums[s * c:(s + 1) * c], a)
    a = jnp.where(col_a <= row_a, a, 0.0)
    v_bd = jnp.concatenate([_head_lanes(vb, hd, hb) for hd in range(hb)], axis=0)
    o = jnp.dot(a.astype(BF16), v_bd, preferred_element_type=F32)
    q_abs = (q * jnp.exp2(b)).astype(BF16)
    k_end = (k * jnp.exp2(b_last - b)).astype(BF16)
    decay = jnp.exp2(b_last)
    from_state = []
    for hd in range(hb):
        keys = slice(hd * dk, (hd + 1) * dk)
        st = st_ref[hd]
        from_state.append(lax.dot_general(q_abs[:, keys], st.astype(BF16), _NT,
                                          preferred_element_type=F32))
        st_ref[hd] = st * decay[:, keys] + lax.dot_general(
            vb[:, hd * dv:(hd + 1) * dv], k_end[:, keys], _TN, preferred_element_type=F32)
    return o + jnp.concatenate(from_state, axis=1)


def _slabs(ref, r0, c):
    return jnp.concatenate([ref[s, pl.ds(r0, c), :] for s in range(ref.shape[0])], axis=-1)


def _rec_kernel(*refs, mode, c, nchunk, has_state):
    if has_state:
        (q_ref, k_ref, v_ref, r_ref, aux_ref, gain_ref, ones_ref, s0_ref,
         o_ref, sout_ref, st_ref, b_sc, k_sc) = refs
    else:
        (q_ref, k_ref, v_ref, r_ref, aux_ref, gain_ref, ones_ref,
         o_ref, sout_ref, st_ref, b_sc, k_sc) = refs
        s0_ref = None
    t = pl.program_id(2)
    hb, dv, dk = st_ref.shape

    @pl.when(t == 0)
    def _():
        for hd in range(hb):
            if has_state:
                st_ref[hd] = s0_ref[0, hd].T
            else:
                st_ref[hd] = jnp.zeros((dv, dk), F32)

    def chunk(ci, carry):
        r0 = pl.multiple_of(ci * c, c)
        xq = _slabs(q_ref, r0, c)
        xk = _slabs(k_ref, r0, c)
        xv = _slabs(v_ref, r0, c)
        xr = _slabs(r_ref, r0, c)
        if mode == "hgrn":
            one_m_lb, log_lb, log_1m_lb = (
                jnp.concatenate([aux_ref[hd, r:r + 1, :] for hd in range(hb)], axis=1)
                for r in range(3))
            q = xq * jax.nn.sigmoid(xq)
            k = one_m_lb * jax.nn.sigmoid(-xk)
            gated = log_1m_lb + _log_sigmoid(xk)
            g = jnp.maximum(log_lb, gated) + _log1p_exp_neg(jnp.abs(log_lb - gated))
            log2k = (gated - xk) * LOG2E
            out_gate = jax.nn.sigmoid(xr)
        else:
            q = xq * (dk ** -0.5)
            k = xk
            g = _slabs(aux_ref, r0, c)
            log2k = None
            out_gate = xr * jax.nn.sigmoid(xr)
        o = _chunk_step(q, k, xv, g, ones_ref[...], st_ref, b_sc, k_sc, c, log2k)
        for hd in range(hb):
            lanes = slice(hd * dv, (hd + 1) * dv)
            o_ref[pl.ds(r0, c), lanes] = (
                _rms(o[:, lanes], gain_ref[hd]) * out_gate[:, lanes]).astype(o_ref.dtype)
        return carry

    lax.fori_loop(0, nchunk, chunk, 0)

    @pl.when(t == pl.num_programs(2) - 1)
    def _():
        for hd in range(hb):
            sout_ref[0, hd] = st_ref[hd].T


def _recurrence(z3, aux, gain, s0, *, mode, batch, seq, heads, dk, dv, q_off, k_off, v_off, r_off,
                aux_is_slab, heads_per_step):
    c = min(CHUNK, seq)
    tb = _tile(seq, 256) if seq >= 256 else seq
    nchunk = tb // c
    nt = seq // tb
    kq, kv = dk // LANE, dv // LANE
    has_state = s0 is not None
    hb = min(heads, heads_per_step)
    assert heads % hb == 0

    def slab_spec(per_head, off):
        width = per_head * hb
        assert off % width == 0
        return pl.BlockSpec((width, tb, LANE), lambda b, h, t: (off // width + h, b * nt + t, 0))

    in_specs = [slab_spec(kq, q_off), slab_spec(kq, k_off), slab_spec(kv, v_off),
                slab_spec(kv, r_off)]
    if aux_is_slab:
        in_specs.append(slab_spec(kq, 0))
    else:
        in_specs.append(pl.BlockSpec((hb, 8, dk), lambda b, h, t: (h, 0, 0)))
    in_specs.append(pl.BlockSpec((hb, 1, dv), lambda b, h, t: (h, 0, 0)))
    in_specs.append(pl.BlockSpec((hb * dk, hb * c), lambda b, h, t: (0, 0)))
    ones_bd = jnp.kron(jnp.eye(hb, dtype=F32), jnp.ones((dk, c), F32)).astype(BF16)
    args = [z3, z3, z3, z3, aux, gain, ones_bd]
    if has_state:
        in_specs.append(pl.BlockSpec((1, hb, dk, dv), lambda b, h, t: (b, h, 0, 0)))
        args.append(s0)
    kern = functools.partial(_rec_kernel, mode=mode, c=c, nchunk=nchunk, has_state=has_state)
    return pl.pallas_call(
        kern,
        grid=(batch, heads // hb, nt),
        in_specs=in_specs,
        out_specs=[
            pl.BlockSpec((tb, hb * dv), lambda b, h, t: (b * nt + t, h)),
            pl.BlockSpec((1, hb, dk, dv), lambda b, h, t: (b, h, 0, 0)),
        ],
        out_shape=[
            jax.ShapeDtypeStruct((batch * seq, heads * dv), BF16),
            jax.ShapeDtypeStruct((batch, heads, dk, dv), F32),
        ],
        scratch_shapes=[pltpu.VMEM((hb, dv, dk), F32), pltpu.VMEM((c, hb * dk), F32),
                        pltpu.VMEM((c, hb * dk), F32)],
        compiler_params=_params(("parallel", "parallel", "arbitrary"), 40),
        name=mode + "_recurrence",
    )(*args)


def _out_proj_kernel(oh_ref, og_ref, w1_ref, w2_ref, h_ref, out_ref):
    acc = jnp.dot(oh_ref[...], w1_ref[...], preferred_element_type=F32)
    acc = acc + jnp.dot(og_ref[...], w2_ref[...], preferred_element_type=F32)
    out_ref[...] = h_ref[...] + acc


def _out_proj(o_h, o_g, w_top, w_bot, h):
    n, d = h.shape
    tm = _tile(n, 512)
    tn = _tile(d, 1024)
    return pl.pallas_call(
        _out_proj_kernel,
        grid=(n // tm, d // tn),
        in_specs=[
            pl.BlockSpec((tm, o_h.shape[1]), lambda i, j: (i, 0)),
            pl.BlockSpec((tm, o_g.shape[1]), lambda i, j: (i, 0)),
            pl.BlockSpec((w_top.shape[0], tn), lambda i, j: (0, j)),
            pl.BlockSpec((w_bot.shape[0], tn), lambda i, j: (0, j)),
            pl.BlockSpec((tm, tn), lambda i, j: (i, j)),
        ],
        out_specs=pl.BlockSpec((tm, tn), lambda i, j: (i, j)),
        out_shape=jax.ShapeDtypeStruct((n, d), F32),
        compiler_params=_params(("parallel", "arbitrary"), 40),
        name="out_proj",
    )(o_h, o_g, w_top, w_bot, h)


def _top_values(x):
    rows = lax.broadcasted_iota(jnp.int32, (TOPK, LANE), 0)
    vals = jnp.full((TOPK, LANE), NEG_INF, F32)
    cnts = jnp.zeros((TOPK, LANE), F32)
    for a in range(TOPK):
        m = jnp.max(x, axis=0, keepdims=True)
        eq = x == m
        n_eq = jnp.sum(jnp.where(eq, 1.0, 0.0), axis=0, keepdims=True)
        vals = jnp.where(rows == a, m, vals)
        cnts = jnp.where(rows == a, n_eq, cnts)
        x = jnp.where(eq, NEG_INF, x)
    return vals, cnts


def _candidate_sums(v0, n0, v1, n1):
    half = TOPK // 2
    r16 = lax.broadcasted_iota(jnp.int32, (TOPK, LANE), 0)
    r8 = lax.broadcasted_iota(jnp.int32, (half, LANE), 0)
    pieces = [
        (v0[0:1] + v1, n0[0:1] * n1, None),
        (v0[1:2] + v1[:half], n0[1:2] * n1[:half], None),
        (v0[2:3] + v1[:half], n0[2:3] * n1[:half], r8 < 5),
        (v0[3:4] + v1[:half], n0[3:4] * n1[:half], r8 < 4),
        (v0 + v1[0:1], n0 * n1[0:1], r16 >= 4),
        (v0[:half] + v1[1:2], n0[:half] * n1[1:2], r8 >= 4),
        (v0[:half] + v1[2:3], n0[:half] * n1[2:3], r8 == 4),
    ]
    cand = jnp.concatenate([c if m is None else jnp.where(m, c, NEG_INF) for c, _, m in pieces],
                           axis=0)
    mult = jnp.concatenate([n if m is None else jnp.where(m, n, 0.0) for _, n, m in pieces],
                           axis=0)
    return cand, mult


def _peer_score_kernel(h_ref, g_ref, wq_ref, keys_ref, xn_ref, s1_ref, e1_ref, tr_ref, er_ref,
                       sc_ref):
    @pl.when(pl.program_id(1) == 0)
    def _():
        def norm(rs):
            xn_ref[rs, :] = _rms(h_ref[rs, :], g_ref[...]).astype(BF16)

        _rows_loop(h_ref.shape[0], norm)

    pq =jnp.dot(xn_ref[...], wq_ref[...], preferred_element_type=F32)
    for cset in range(2):
        qc = pq[:, cset * LANE:(cset + 1) * LANE].astype(BF16)
        sc_ref[cset] = lax.dot_general(keys_ref[cset], qc, _NT, preferred_element_type=F32)

    def lanes(ci, carry):
        l0 = pl.multiple_of(ci * LANE, LANE)
        s0 = sc_ref[0, :, pl.ds(l0, LANE)]
        s1 = sc_ref[1, :, pl.ds(l0, LANE)]
        v0, n0 = _top_values(s0)
        v1, n1 = _top_values(s1)
        cand, mult = _candidate_sums(v0, n0, v1, n1)
        top = v0[0:1] + v1[0:1]
        tau = jnp.full((1, LANE), NEG_INF, F32)
        seen = jnp.zeros((1, LANE), F32)
        x = cand
        for _ in range(TOPK):
            m = jnp.max(x, axis=0, keepdims=True)
            eq = x == m
            tau = jnp.where(seen < TOPK, m, tau)
            seen = seen + jnp.sum(jnp.where(eq, mult, 0.0), axis=0, keepdims=True)
            x = jnp.where(eq, NEG_INF, x)
        z = jnp.sum(jnp.where(cand >= tau, mult * jnp.exp(cand - top), 0.0), axis=0,
                    keepdims=True)
        s1_ref[0, :, pl.ds(l0, LANE)] = s1
        e1_ref[0, :, pl.ds(l0, LANE)] = jnp.exp(s1 - v1[0:1])
        tr_ref[0, :, pl.ds(l0, LANE)] = tau - s0
        er_ref[0, :, pl.ds(l0, LANE)] = jnp.exp(s0 - v0[0:1]) / z
        return carry

    lax.fori_loop(0, sc_ref.shape[2] // LANE, lanes, 0)


def _peer_scores(h, g, wq, keys):
    n, d = h.shape
    heads = wq.shape[1] // (2 * LANE)
    tm = _tile(n, 512)
    aux_spec = pl.BlockSpec((1, LANE, tm), lambda i, hd: (hd, 0, i))
    aux_shape = jax.ShapeDtypeStruct((heads, LANE, n), F32)
    return pl.pallas_call(
        _peer_score_kernel,
        grid=(n // tm, heads),
        in_specs=[
            pl.BlockSpec((tm, d), lambda i, hd: (i, 0)),
            pl.BlockSpec((1, d), lambda i, hd: (0, 0)),
            pl.BlockSpec((d, 2 * LANE), lambda i, hd: (0, hd)),
            pl.BlockSpec((2, LANE, LANE), lambda i, hd: (0, 0, 0)),
        ],
        out_specs=[pl.BlockSpec((tm, d), lambda i, hd: (i, 0)), aux_spec, aux_spec, aux_spec,
                   aux_spec],
        out_shape=[jax.ShapeDtypeStruct((n, d), BF16), aux_shape, aux_shape, aux_shape,
                   aux_shape],
        scratch_shapes=[pltpu.VMEM((2, LANE, tm), F32)],
        compiler_params=_params(("parallel", "arbitrary"), 40),
        name="peer_scores",
    )(h, g, wq, keys)


def _peer_dense_kernel(xn_ref, u_ref, v_ref, s1_ref, e1_ref, tr_ref, er_ref, y_ref):
    e = pl.program_id(1)
    heads = s1_ref.shape[0]
    tiles = u_ref.shape[0] // LANE

    @pl.when(e == 0)
    def _():
        y_ref[...] = jnp.zeros(y_ref.shape, F32)

    xn = xn_ref[...]
    gated = []
    for grp in range(tiles // GROUP):
        rows = slice(grp * GROUP * LANE, (grp + 1) * GROUP * LANE)
        act = lax.dot_general(xn, u_ref[rows, :], _NT, preferred_element_type=F32)
        for ii in range(GROUP):
            i = e * tiles + grp * GROUP + ii
            wt = jnp.zeros((LANE, xn_ref.shape[0]), F32)
            for hd in range(heads):
                thr = tr_ref[hd, pl.ds(i, 1), :]
                scale = er_ref[hd, pl.ds(i, 1), :]
                wt = wt + jnp.where(s1_ref[hd] >= thr, e1_ref[hd] * scale, 0.0)
            a = act[:, ii * LANE:(ii + 1) * LANE]
            gelu = 0.5 * a * (1.0 + lax.erf(a * 0.7071067811865476))
            gated.append((wt.T * gelu).astype(BF16))
    y_ref[...] += jnp.dot(jnp.concatenate(gated, axis=1), v_ref[...],
                          preferred_element_type=F32)


def _peer_dense(xn, u, v, s1, e1, tr, er):
    n, d = xn.shape
    n_exp = u.shape[0]
    heads = s1.shape[0]
    tn = _tile(n, 512)
    te = 512
    once = pl.Buffered(1)
    aux_spec = pl.BlockSpec((heads, LANE, tn), lambda t, e: (0, 0, t), pipeline_mode=once)
    return pl.pallas_call(
        _peer_dense_kernel,
        grid=(n // tn, n_exp // te),
        in_specs=[
            pl.BlockSpec((tn, d), lambda t, e: (t, 0), pipeline_mode=once),
            pl.BlockSpec((te, d), lambda t, e: (e, 0)),
            pl.BlockSpec((te, d), lambda t, e: (e, 0)),
            aux_spec, aux_spec, aux_spec, aux_spec,
        ],
        out_specs=pl.BlockSpec((tn, d), lambda t, e: (t, 0)),
        out_shape=jax.ShapeDtypeStruct((n, d), F32),
        compiler_params=_params(("parallel", "arbitrary"), 56),
        name="peer_dense",
    )(xn, u, v, s1, e1, tr, er)


def _add_norm_kernel(h_ref, y_ref, g_ref, h2_ref, xn_ref):
    def norm(rs):
        h2 = h_ref[rs, :] + y_ref[rs, :]
        h2_ref[rs, :] = h2
        xn_ref[rs, :] = _rms(h2, g_ref[...]).astype(BF16)

    _rows_loop(h_ref.shape[0], norm)


def _add_norm(h, y, g):
    n, d = h.shape
    tm = _tile(n, 256)
    row = pl.BlockSpec((tm, d), lambda i: (i, 0))
    return pl.pallas_call(
        _add_norm_kernel,
        grid=(n // tm,),
        in_specs=[row, row, pl.BlockSpec((1, d), lambda i: (0, 0))],
        out_specs=[row, row],
        out_shape=[jax.ShapeDtypeStruct((n, d), F32), jax.ShapeDtypeStruct((n, d), BF16)],
        compiler_params=_params(("parallel",), 40),
        name="add_norm",
    )(h, y, g)


def _ple_kernel(xn_ref, wg_ref, h2_ref, p_ref, wp_ref, out_ref):
    gate = jax.nn.sigmoid(jnp.dot(xn_ref[...], wg_ref[...], preferred_element_type=F32))
    proj = jnp.dot(p_ref[...].astype(BF16), wp_ref[...], preferred_element_type=F32)
    out_ref[...] = h2_ref[...] + gate * proj


def _ple(xn, h2, wg, p, wp):
    n, d = h2.shape
    tm = _tile(n, 1024)
    tn = _tile(d, 512)
    return pl.pallas_call(
        _ple_kernel,
        grid=(n // tm, d // tn),
        in_specs=[
            pl.BlockSpec((tm, d), lambda i, j: (i, 0)),
            pl.BlockSpec((d, tn), lambda i, j: (0, j)),
            pl.BlockSpec((tm, tn), lambda i, j: (i, j)),
            pl.BlockSpec((tm, p.shape[1]), lambda i, j: (i, 0)),
            pl.BlockSpec((p.shape[1], tn), lambda i, j: (0, j)),
        ],
        out_specs=pl.BlockSpec((tm, tn), lambda i, j: (i, j)),
        out_shape=jax.ShapeDtypeStruct((n, d), F32),
        compiler_params=_params(("parallel", "arbitrary"), 40),
        name="ple",
    )(xn, wg, h2, p, wp)


def _final_norm_kernel(x_ref, g_ref, o_ref):
    def norm(rs):
        o_ref[rs, :] = _rms(x_ref[rs, :], g_ref[...])

    _rows_loop(x_ref.shape[0], norm)


def _final_norm(h, g):
    n, d = h.shape
    tm = _tile(n, 512)
    return pl.pallas_call(
        _final_norm_kernel,
        grid=(n // tm,),
        in_specs=[pl.BlockSpec((tm, d), lambda i: (i, 0)), pl.BlockSpec((1, d), lambda i: (0, 0))],
        out_specs=pl.BlockSpec((tm, d), lambda i: (i, 0)),
        out_shape=jax.ShapeDtypeStruct((n, d), F32),
        compiler_params=_params(("parallel",), 40),
        name="final_norm",
    )(h, g)


def _prepare(g_mix, w_in, hgrn_lb_logits, hgrn_norm, w_gla_a2, b_gla_a, gla_norm, w_out, g_ffn,
             w_peer_q, peer_sub_keys, peer_u, peer_v, g_ple, w_ple_gate, w_ple_proj):
    depth = w_in.shape[0]
    hh, hdv = hgrn_norm.shape[1:]
    hw = hh * hdv
    low_rank = w_gla_a2.shape[1]
    main = w_in.shape[2] - low_rank
    soft = jax.nn.softmax(hgrn_lb_logits.astype(F32), axis=0)
    lb = jnp.maximum(jnp.cumsum(soft, axis=0) - soft[0], 0.0).reshape(depth, hh, 1, hw // hh)
    lb_rows = jnp.concatenate([1.0 - lb, jnp.log(lb), jnp.log1p(-lb),
                               jnp.zeros((depth, hh, 5, hw // hh), F32)], axis=2)
    w_in_bf16 = w_in.astype(BF16)
    layers = []
    for i in range(depth):
        layers.append(dict(
            g_mix=g_mix[i][None], g_ffn=g_ffn[i][None], g_ple=g_ple[i][None],
            w_in=w_in_bf16, main_cols=main,
            w_lr=jnp.pad(w_in[i, :, main:], ((0, 0), (0, LANE - low_rank))).astype(BF16),
            w_a2=jnp.pad(w_gla_a2[i], ((0, LANE - low_rank), (0, 0))).astype(BF16),
            b_a=b_gla_a[i][None],
            lb_rows=lb_rows[i],
            hgrn_norm=hgrn_norm[i][:, None, :], gla_norm=gla_norm[i][:, None, :],
            w_out_top=w_out[i, :hw].astype(BF16), w_out_bot=w_out[i, hw:].astype(BF16),
            w_peer_q=w_peer_q[i].astype(BF16), keys=peer_sub_keys[i].astype(BF16),
            u=peer_u[i].astype(BF16), v=peer_v[i].astype(BF16),
            w_ple_gate=w_ple_gate[i].astype(BF16), w_ple_proj=w_ple_proj[i].astype(BF16),
        ))
    return layers


def _trunk(x, p, st_h, st_g, layers, g_final):
    batch, seq, d = x.shape
    n = batch * seq
    h = x.reshape(n, d)
    new_h, new_g = [], []
    for i, lw in enumerate(layers):
        hh, _, hdk = lw["hgrn_norm"].shape
        gh, _, gdv = lw["gla_norm"].shape
        hw = hh * hdk
        gkw = lw["w_a2"].shape[1]
        gdk = gkw // gh
        z3, la3 = _in_proj(h, lw["g_mix"], lw["w_in"], i, lw["main_cols"], lw["w_lr"],
                           lw["w_a2"], lw["b_a"])
        s_hw = hw // LANE
        o_h, s_h = _recurrence(
            z3, lw["lb_rows"], lw["hgrn_norm"], None if st_h is None else st_h[i],
            mode="hgrn", batch=batch, seq=seq, heads=hh, dk=hdk, dv=hdk,
            q_off=0, k_off=s_hw, v_off=2 * s_hw, r_off=3 * s_hw, aux_is_slab=False,
            heads_per_step=4)
        g0 = 4 * s_hw
        s_kw = gkw // LANE
        s_vw = gh * gdv // LANE
        o_g, s_g = _recurrence(
            z3, la3, lw["gla_norm"], None if st_g is None else st_g[i],
            mode="gla", batch=batch, seq=seq, heads=gh, dk=gdk, dv=gdv,
            q_off=g0, k_off=g0 + s_kw, v_off=g0 + 2 * s_kw, r_off=g0 + 2 * s_kw + s_vw,
            aux_is_slab=True, heads_per_step=2)
        h = _out_proj(o_h, o_g, lw["w_out_top"], lw["w_out_bot"], h)
        xn, s1, e1, tr, er = _peer_scores(h, lw["g_ffn"], lw["w_peer_q"], lw["keys"])
        y = _peer_dense(xn, lw["u"], lw["v"], s1, e1, tr, er)
        h2, xn = _add_norm(h, y, lw["g_ple"])
        h = _ple(xn, h2, lw["w_ple_gate"], p[i].reshape(n, -1), lw["w_ple_proj"])
        new_h.append(s_h)
        new_g.append(s_g)
    out = _final_norm(h, g_final[None])
    return out.reshape(batch, seq, d), jnp.stack(new_h), jnp.stack(new_g)


def kernel(x_prompt, x_sample, state_hgrn, state_gla, p_prompt, p_sample, g_mix, w_in,
           hgrn_lb_logits, hgrn_norm, w_gla_a2, b_gla_a, gla_norm, w_out, g_ffn, w_peer_q,
           peer_sub_keys, peer_u, peer_v, g_ple, w_ple_gate, w_ple_proj, g_final):
    layers = _prepare(g_mix, w_in, hgrn_lb_logits, hgrn_norm, w_gla_a2, b_gla_a, gla_norm, w_out,
                      g_ffn, w_peer_q, peer_sub_keys, peer_u, peer_v, g_ple, w_ple_gate,
                      w_ple_proj)
    y_p, h_p, g_p = _trunk(x_prompt, p_prompt, None, None, layers, g_final)
    y_s, h_s, g_s = _trunk(x_sample, p_sample, state_hgrn, state_gla, layers, g_final)
    return y_p, y_s, h_p, g_p, h_s, g_s
```

```python
import functools

import jax
import jax.numpy as jnp
from jax import lax
from jax.experimental import pallas as pl
from jax.experimental.pallas import tpu as pltpu

F32 = jnp.float32
BF16 = jnp.bfloat16
EPS = 1e-6
LANE = 128
SUB = 16
CHUNK = 64
TOPK = 16
GATE_NORMALIZER = 16.0
GROUP = 2
NEG_INF = float("-inf")
LOG2E = 1.4426950408889634
MIB = 1024 * 1024

_NT = (((1,), (1,)), ((), ()))
_TN = (((0,), (0,)), ((), ()))


def _params(sem, vmem_mib):
    return pltpu.CompilerParams(dimension_semantics=sem, vmem_limit_bytes=vmem_mib * MIB)


def _rms(x, g):
    return x * lax.rsqrt(jnp.mean(x * x, axis=-1, keepdims=True) + EPS) * g


def _log1p_exp_neg(x):
    return jnp.log(1.0 + jnp.exp(-x))


def _log_sigmoid(x):
    return jnp.minimum(x, 0.0) - _log1p_exp_neg(jnp.abs(x))


def _tile(n, pref):
    return pref if n % pref == 0 else n


def _rows_loop(n_rows, fn, rows=128):
    rows = min(rows, n_rows)

    def body(i, carry):
        fn(pl.ds(pl.multiple_of(i * rows, rows), rows))
        return carry

    lax.fori_loop(0, n_rows // rows, body, 0)


def _in_proj_kernel(x_ref, g_ref, w_ref, wlr_ref, wa2_ref, ba_ref, z_ref, la_ref, xn_ref):
    @pl.when(pl.program_id(1) == 0)
    def _():
        def norm(rs):
            xn_ref[rs, :] = _rms(x_ref[rs, :], g_ref[...]).astype(BF16)

        _rows_loop(x_ref.shape[0], norm)
        glr = jnp.dot(xn_ref[...], wlr_ref[...], preferred_element_type=F32)
        ga = jnp.dot(glr.astype(BF16), wa2_ref[...], preferred_element_type=F32) + ba_ref[...]
        la = _log_sigmoid(ga) * (1.0 / GATE_NORMALIZER)
        for s in range(la_ref.shape[0]):
            la_ref[s] = la[:, s * LANE:(s + 1) * LANE]

    res = jnp.dot(xn_ref[...], w_ref[...], preferred_element_type=F32)
    for s in range(z_ref.shape[0]):
        z_ref[s] = res[:, s * LANE:(s + 1) * LANE]


def _in_proj(h, g, w_all, layer, cols, w_lr, w_a2, b_a):
    n, d = h.shape
    kw = w_a2.shape[1]
    tm = _tile(n, 512)
    tn = _tile(cols, 1024)
    return pl.pallas_call(
        _in_proj_kernel,
        grid=(n // tm, cols // tn),
        in_specs=[
            pl.BlockSpec((tm, d), lambda i, j: (i, 0)),
            pl.BlockSpec((1, d), lambda i, j: (0, 0)),
            pl.BlockSpec((None, d, tn), lambda i, j: (layer, 0, j)),
            pl.BlockSpec((d, LANE), lambda i, j: (0, 0)),
            pl.BlockSpec((LANE, kw), lambda i, j: (0, 0)),
            pl.BlockSpec((1, kw), lambda i, j: (0, 0)),
        ],
        out_specs=[
            pl.BlockSpec((tn // LANE, tm, LANE), lambda i, j: (j, i, 0)),
            pl.BlockSpec((kw // LANE, tm, LANE), lambda i, j: (0, i, 0)),
        ],
        out_shape=[
            jax.ShapeDtypeStruct((cols // LANE, n, LANE), F32),
            jax.ShapeDtypeStruct((kw // LANE, n, LANE), F32),
        ],
        scratch_shapes=[pltpu.VMEM((tm, d), BF16)],
        compiler_params=_params(("parallel", "arbitrary"), 52),
        name="in_proj",
    )(h, g, w_all, w_lr, w_a2, b_a)


def _head_lanes(x, hd, hb):
    w = x.shape[1] // hb
    zero = jnp.zeros((x.shape[0], w), x.dtype)
    return jnp.concatenate([x[:, j * w:(j + 1) * w] if j == hd else zero for j in range(hb)],
                           axis=1)


def _rows_per_block(ref, s, nsub):
    return jnp.concatenate([jnp.broadcast_to(ref[pl.ds(j * SUB + s, 1), :], (SUB, ref.shape[1]))
                            for j in range(nsub)], axis=0)


def _chunk_step(q, k, v, g, ones_bd, st_ref, b_ref, k_ref, c, log2k=None):
    hb, dv, dk = st_ref.shape
    nsub = c // SUB
    row = lax.broadcasted_iota(jnp.int32, (c, c), 0)
    col = lax.broadcasted_iota(jnp.int32, (c, c), 1)
    tril = jnp.where(row >= col, 1.0, 0.0).astype(F32)
    b = jnp.dot(tril, g * LOG2E, precision=lax.Precision.HIGHEST, preferred_element_type=F32)
    b_ref[...] = b
    k_ref[...] = k if log2k is None else log2k - b
    vb = v.astype(BF16)
    b_last = b_ref[pl.ds(c - 1, 1), :]
    a_rows = [jnp.zeros((SUB, hb * c), F32)]
    for blk in range(1, nsub):
        lo = blk * SUB
        ref = b_ref[pl.ds(lo - 1, 1), :]
        q_dec = (q[lo:lo + SUB] * jnp.exp2(b[lo:lo + SUB] - ref)).astype(BF16)
        k_dec = jnp.concatenate(
            [(k[:lo] * jnp.exp2(ref - b[:lo])).astype(BF16), jnp.zeros((c - lo, hb * dk), BF16)],
            axis=0)
        k_bd = jnp.concatenate([_head_lanes(k_dec, hd, hb) for hd in range(hb)], axis=0)
        a_rows.append(lax.dot_general(q_dec, k_bd, _NT, preferred_element_type=F32))
    a = jnp.concatenate(a_rows, axis=0)
    prods = []
    for s in range(SUB):
        if log2k is None:
            k_s = _rows_per_block(k_ref, s, nsub)
            prods.append((q * k_s * jnp.exp2(b - _rows_per_block(b_ref, s, nsub))).astype(BF16))
        else:
            prods.append((q * jnp.exp2(b + _rows_per_block(k_ref, s, nsub))).astype(BF16))
    sums = jnp.dot(jnp.concatenate(prods, axis=0), ones_bd, preferred_element_type=F32)
    row_a = lax.broadcasted_iota(jnp.int32, (c, hb * c), 0)
    col_a = lax.broadcasted_iota(jnp.int32, (c, hb * c), 1) % c
    block_start = (row_a // SUB) * SUB
    for s in range(SUB):
        a = jnp.where(col_a == block_start + s, sums[s * c:(s + 1) * c], a)
    a = jnp.where(col_a <= row_a, a, 0.0)
    v_bd = jnp.concatenate([_head_lanes(vb, hd, hb) for hd in range(hb)], axis=0)
    o = jnp.dot(a.astype(BF16), v_bd, preferred_element_type=F32)
    q_abs = (q * jnp.exp2(b)).astype(BF16)
    k_end = (k * jnp.exp2(b_last - b)).astype(BF16)
    decay = jnp.exp2(b_last)
    from_state = []
    for hd in range(hb):
        keys = slice(hd * dk, (hd + 1) * dk)
        st = st_ref[hd]
        from_state.append(lax.dot_general(q_abs[:, keys], st.astype(BF16), _NT,
                                          preferred_element_type=F32))
        st_ref[hd] = st * decay[:, keys] + lax.dot_general(
            vb[:, hd * dv:(hd + 1) * dv], k_end[:, keys], _TN, preferred_element_type=F32)
    return o + jnp.concatenate(from_state, axis=1)


def _slabs(ref, r0, c):
    return jnp.concatenate([ref[s, pl.ds(r0, c), :] for s in range(ref.shape[0])], axis=-1)


def _rec_kernel(*refs, mode, c, nchunk, has_state):
    if has_state:
        (q_ref, k_ref, v_ref, r_ref, aux_ref, gain_ref, ones_ref, s0_ref,
         o_ref, sout_ref, st_ref, b_sc, k_sc) = refs
    else:
        (q_ref, k_ref, v_ref, r_ref, aux_ref, gain_ref, ones_ref,
         o_ref, sout_ref, st_ref, b_sc, k_sc) = refs
        s0_ref = None
    t = pl.program_id(2)
    hb, dv, dk = st_ref.shape

    @pl.when(t == 0)
    def _():
        for hd in range(hb):
            if has_state:
                st_ref[hd] = s0_ref[0, hd].T
            else:
                st_ref[hd] = jnp.zeros((dv, dk), F32)

    def chunk(ci, carry):
        r0 = pl.multiple_of(ci * c, c)
        xq = _slabs(q_ref, r0, c)
        xk = _slabs(k_ref, r0, c)
        xv = _slabs(v_ref, r0, c)
        xr = _slabs(r_ref, r0, c)
        if mode == "hgrn":
            one_m_lb, log_lb, log_1m_lb = (
                jnp.concatenate([aux_ref[hd, r:r + 1, :] for hd in range(hb)], axis=1)
                for r in range(3))
            q = xq * jax.nn.sigmoid(xq)
            k = one_m_lb * jax.nn.sigmoid(-xk)
            gated = log_1m_lb + _log_sigmoid(xk)
            g = jnp.maximum(log_lb, gated) + _log1p_exp_neg(jnp.abs(log_lb - gated))
            log2k = (gated - xk) * LOG2E
            out_gate = jax.nn.sigmoid(xr)
        else:
            q = xq * (dk ** -0.5)
            k = xk
            g = _slabs(aux_ref, r0, c)
            log2k = None
            out_gate = xr * jax.nn.sigmoid(xr)
        o = _chunk_step(q, k, xv, g, ones_ref[...], st_ref, b_sc, k_sc, c, log2k)
        for hd in range(hb):
            lanes = slice(hd * dv, (hd + 1) * dv)
            o_ref[pl.ds(r0, c), lanes] = (
                _rms(o[:, lanes], gain_ref[hd]) * out_gate[:, lanes]).astype(o_ref.dtype)
        return carry

    lax.fori_loop(0, nchunk, chunk, 0)

    @pl.when(t == pl.num_programs(2) - 1)
    def _():
        for hd in range(hb):
            sout_ref[0, hd] = st_ref[hd].T


def _recurrence(z3, aux, gain, s0, *, mode, batch, seq, heads, dk, dv, q_off, k_off, v_off, r_off,
                aux_is_slab, heads_per_step):
    c = min(CHUNK, seq)
    tb = _tile(seq, 256) if seq >= 256 else seq
    nchunk = tb // c
    nt = seq // tb
    kq, kv = dk // LANE, dv // LANE
    has_state = s0 is not None
    hb = min(heads, heads_per_step)
    assert heads % hb == 0

    def slab_spec(per_head, off):
        width = per_head * hb
        assert off % width == 0
        return pl.BlockSpec((width, tb, LANE), lambda b, h, t: (off // width + h, b * nt + t, 0))

    in_specs = [slab_spec(kq, q_off), slab_spec(kq, k_off), slab_spec(kv, v_off),
                slab_spec(kv, r_off)]
    if aux_is_slab:
        in_specs.append(slab_spec(kq, 0))
    else:
        in_specs.append(pl.BlockSpec((hb, 8, dk), lambda b, h, t: (h, 0, 0)))
    in_specs.append(pl.BlockSpec((hb, 1, dv), lambda b, h, t: (h, 0, 0)))
    in_specs.append(pl.BlockSpec((hb * dk, hb * c), lambda b, h, t: (0, 0)))
    ones_bd = jnp.kron(jnp.eye(hb, dtype=F32), jnp.ones((dk, c), F32)).astype(BF16)
    args = [z3, z3, z3, z3, aux, gain, ones_bd]
    if has_state:
        in_specs.append(pl.BlockSpec((1, hb, dk, dv), lambda b, h, t: (b, h, 0, 0)))
        args.append(s0)
    kern = functools.partial(_rec_kernel, mode=mode, c=c, nchunk=nchunk, has_state=has_state)
    return pl.pallas_call(
        kern,
        grid=(batch, heads // hb, nt),
        in_specs=in_specs,
        out_specs=[
            pl.BlockSpec((tb, hb * dv), lambda b, h, t: (b * nt + t, h)),
            pl.BlockSpec((1, hb, dk, dv), lambda b, h, t: (b, h, 0, 0)),
        ],
        out_shape=[
            jax.ShapeDtypeStruct((batch * seq, heads * dv), BF16),
            jax.ShapeDtypeStruct((batch, heads, dk, dv), F32),
        ],
        scratch_shapes=[pltpu.VMEM((hb, dv, dk), F32), pltpu.VMEM((c, hb * dk), F32),
                        pltpu.VMEM((c, hb * dk), F32)],
        compiler_params=_params(("parallel", "parallel", "arbitrary"), 40),
        name=mode + "_recurrence",
    )(*args)


def _out_proj_kernel(oh_ref, og_ref, w1_ref, w2_ref, h_ref, out_ref):
    acc = jnp.dot(oh_ref[...], w1_ref[...], preferred_element_type=F32)
    acc = acc + jnp.dot(og_ref[...], w2_ref[...], preferred_element_type=F32)
    out_ref[...] = h_ref[...] + acc


def _out_proj(o_h, o_g, w_top, w_bot, h):
    n, d = h.shape
    tm = _tile(n, 512)
    tn = _tile(d, 1024)
    return pl.pallas_call(
        _out_proj_kernel,
        grid=(n // tm, d // tn),
        in_specs=[
            pl.BlockSpec((tm, o_h.shape[1]), lambda i, j: (i, 0)),
            pl.BlockSpec((tm, o_g.shape[1]), lambda i, j: (i, 0)),
            pl.BlockSpec((w_top.shape[0], tn), lambda i, j: (0, j)),
            pl.BlockSpec((w_bot.shape[0], tn), lambda i, j: (0, j)),
            pl.BlockSpec((tm, tn), lambda i, j: (i, j)),
        ],
        out_specs=pl.BlockSpec((tm, tn), lambda i, j: (i, j)),
        out_shape=jax.ShapeDtypeStruct((n, d), F32),
        compiler_params=_params(("parallel", "arbitrary"), 40),
        name="out_proj",
    )(o_h, o_g, w_top, w_bot, h)


def _sorting_network(n):
    pairs = []
    p = 1
    while p < n:
        k = p
        while k >= 1:
            for j in range(k % p, n - k, 2 * k):
                for i in range(min(k, n - j - k)):
                    if (i + j) // (2 * p) == (i + j + k) // (2 * p):
                        pairs.append((i + j, i + j + k))
            k //= 2
        p *= 2
    return pairs


def _top_values(x):
    groups = x.shape[0] // 8
    lists = [x[8 * v:8 * (v + 1)] for v in range(groups)]
    for i, j in _sorting_network(groups):
        hi, lo = jnp.maximum(lists[i], lists[j]), jnp.minimum(lists[i], lists[j])
        lists[i], lists[j] = hi, lo
    rows = lax.broadcasted_iota(jnp.int32, (TOPK, LANE), 0)
    vals = jnp.full((TOPK, LANE), NEG_INF, F32)
    cnts = jnp.zeros((TOPK, LANE), F32)
    for a in range(TOPK):
        m = jnp.max(lists[0], axis=0, keepdims=True)
        eq = lists[0] == m
        n_eq = jnp.sum(jnp.where(eq, 1.0, 0.0), axis=0, keepdims=True)
        vals = jnp.where(rows == a, m, vals)
        cnts = jnp.where(rows == a, n_eq, cnts)
        depth = min(groups, TOPK - a) - 1
        for v in range(depth):
            lists[v] = jnp.where(eq, lists[v + 1], lists[v])
        if depth < groups:
            lists[depth] = jnp.where(eq, NEG_INF, lists[depth])
    return vals, cnts


def _candidate_sums(v0, n0, v1, n1):
    half = TOPK // 2
    r16 = lax.broadcasted_iota(jnp.int32, (TOPK, LANE), 0)
    r8 = lax.broadcasted_iota(jnp.int32, (half, LANE), 0)
    pieces = [
        (v0[0:1] + v1, n0[0:1] * n1, None),
        (v0[1:2] + v1[:half], n0[1:2] * n1[:half], None),
        (v0[2:3] + v1[:half], n0[2:3] * n1[:half], r8 < 5),
        (v0[3:4] + v1[:half], n0[3:4] * n1[:half], r8 < 4),
        (v0 + v1[0:1], n0 * n1[0:1], r16 >= 4),
        (v0[:half] + v1[1:2], n0[:half] * n1[1:2], r8 >= 4),
        (v0[:half] + v1[2:3], n0[:half] * n1[2:3], r8 == 4),
    ]
    cand = jnp.concatenate([c if m is None else jnp.where(m, c, NEG_INF) for c, _, m in pieces],
                           axis=0)
    mult = jnp.concatenate([n if m is None else jnp.where(m, n, 0.0) for _, n, m in pieces],
                           axis=0)
    return cand, mult


def _peer_score_kernel(h_ref, g_ref, wq_ref, keys_ref, xn_ref, s1_ref, e1_ref, tr_ref, er_ref,
                       sc_ref):
    @pl.when(pl.program_id(1) == 0)
    def _():
        def norm(rs):
            xn_ref[rs, :] = _rms(h_ref[rs, :], g_ref[...]).astype(BF16)

        _rows_loop(h_ref.shape[0], norm)

    pq =jnp.dot(xn_ref[...], wq_ref[...], preferred_element_type=F32)
    for cset in range(2):
        qc = pq[:, cset * LANE:(cset + 1) * LANE].astype(BF16)
        sc_ref[cset] = lax.dot_general(keys_ref[cset], qc, _NT, preferred_element_type=F32)

    def lanes(ci, carry):
        l0 = pl.multiple_of(ci * LANE, LANE)
        s0 = sc_ref[0, :, pl.ds(l0, LANE)]
        s1 = sc_ref[1, :, pl.ds(l0, LANE)]
        v0, n0 = _top_values(s0)
        v1, n1 = _top_values(s1)
        cand, mult = _candidate_sums(v0, n0, v1, n1)
        top = v0[0:1] + v1[0:1]
        tau = jnp.full((1, LANE), NEG_INF, F32)
        seen = jnp.zeros((1, LANE), F32)
        x = cand
        for _ in range(TOPK):
            m = jnp.max(x, axis=0, keepdims=True)
            eq = x == m
            tau = jnp.where(seen < TOPK, m, tau)
            seen = seen + jnp.sum(jnp.where(eq, mult, 0.0), axis=0, keepdims=True)
            x = jnp.where(eq, NEG_INF, x)
        z = jnp.sum(jnp.where(cand >= tau, mult * jnp.exp(cand - top), 0.0), axis=0,
                    keepdims=True)
        s1_ref[0, :, pl.ds(l0, LANE)] = s1
        e1_ref[0, :, pl.ds(l0, LANE)] = jnp.exp(s1 - v1[0:1])
        tr_ref[0, :, pl.ds(l0, LANE)] = tau - s0
        er_ref[0, :, pl.ds(l0, LANE)] = jnp.exp(s0 - v0[0:1]) / z
        return carry

    lax.fori_loop(0, sc_ref.shape[2] // LANE, lanes, 0)


def _peer_scores(h, g, wq, keys):
    n, d = h.shape
    heads = wq.shape[1] // (2 * LANE)
    tm = _tile(n, 512)
    aux_spec = pl.BlockSpec((1, LANE, tm), lambda i, hd: (hd, 0, i))
    aux_shape = jax.ShapeDtypeStruct((heads, LANE, n), F32)
    return pl.pallas_call(
        _peer_score_kernel,
        grid=(n // tm, heads),
        in_specs=[
            pl.BlockSpec((tm, d), lambda i, hd: (i, 0)),
            pl.BlockSpec((1, d), lambda i, hd: (0, 0)),
            pl.BlockSpec((d, 2 * LANE), lambda i, hd: (0, hd)),
            pl.BlockSpec((2, LANE, LANE), lambda i, hd: (0, 0, 0)),
        ],
        out_specs=[pl.BlockSpec((tm, d), lambda i, hd: (i, 0)), aux_spec, aux_spec, aux_spec,
                   aux_spec],
        out_shape=[jax.ShapeDtypeStruct((n, d), BF16), aux_shape, aux_shape, aux_shape,
                   aux_shape],
        scratch_shapes=[pltpu.VMEM((2, LANE, tm), F32)],
        compiler_params=_params(("parallel", "arbitrary"), 40),
        name="peer_scores",
    )(h, g, wq, keys)


def _peer_dense_kernel(xn_ref, u_ref, v_ref, s1_ref, e1_ref, tr_ref, er_ref, y_ref):
    e = pl.program_id(1)
    heads = s1_ref.shape[0]
    tiles = u_ref.shape[0] // LANE

    @pl.when(e == 0)
    def _():
        y_ref[...] = jnp.zeros(y_ref.shape, F32)

    xn = xn_ref[...]
    gated = []
    for grp in range(tiles // GROUP):
        rows = slice(grp * GROUP * LANE, (grp + 1) * GROUP * LANE)
        act = lax.dot_general(xn, u_ref[rows, :], _NT, preferred_element_type=F32)
        for ii in range(GROUP):
            i = e * tiles + grp * GROUP + ii
            wt = jnp.zeros((LANE, xn_ref.shape[0]), F32)
            for hd in range(heads):
                thr = tr_ref[hd, pl.ds(i, 1), :]
                scale = er_ref[hd, pl.ds(i, 1), :]
                wt = wt + jnp.where(s1_ref[hd] >= thr, e1_ref[hd] * scale, 0.0)
            a = act[:, ii * LANE:(ii + 1) * LANE]
            gelu = 0.5 * a * (1.0 + lax.erf(a * 0.7071067811865476))
            gated.append((wt.T * gelu).astype(BF16))
    y_ref[...] += jnp.dot(jnp.concatenate(gated, axis=1), v_ref[...],
                          preferred_element_type=F32)


def _peer_dense(xn, u, v, s1, e1, tr, er):
    n, d = xn.shape
    n_exp = u.shape[0]
    heads = s1.shape[0]
    tn = _tile(n, 512)
    te = 512
    once = pl.Buffered(1)
    aux_spec = pl.BlockSpec((heads, LANE, tn), lambda t, e: (0, 0, t), pipeline_mode=once)
    return pl.pallas_call(
        _peer_dense_kernel,
        grid=(n // tn, n_exp // te),
        in_specs=[
            pl.BlockSpec((tn, d), lambda t, e: (t, 0), pipeline_mode=once),
            pl.BlockSpec((te, d), lambda t, e: (e, 0)),
            pl.BlockSpec((te, d), lambda t, e: (e, 0)),
            aux_spec, aux_spec, aux_spec, aux_spec,
        ],
        out_specs=pl.BlockSpec((tn, d), lambda t, e: (t, 0)),
        out_shape=jax.ShapeDtypeStruct((n, d), F32),
        compiler_params=_params(("parallel", "arbitrary"), 56),
        name="peer_dense",
    )(xn, u, v, s1, e1, tr, er)


def _add_norm_kernel(h_ref, y_ref, g_ref, h2_ref, xn_ref):
    def norm(rs):
        h2 = h_ref[rs, :] + y_ref[rs, :]
        h2_ref[rs, :] = h2
        xn_ref[rs, :] = _rms(h2, g_ref[...]).astype(BF16)

    _rows_loop(h_ref.shape[0], norm)


def _add_norm(h, y, g):
    n, d = h.shape
    tm = _tile(n, 256)
    row = pl.BlockSpec((tm, d), lambda i: (i, 0))
    return pl.pallas_call(
        _add_norm_kernel,
        grid=(n // tm,),
        in_specs=[row, row, pl.BlockSpec((1, d), lambda i: (0, 0))],
        out_specs=[row, row],
        out_shape=[jax.ShapeDtypeStruct((n, d), F32), jax.ShapeDtypeStruct((n, d), BF16)],
        compiler_params=_params(("parallel",), 40),
        name="add_norm",
    )(h, y, g)


def _ple_kernel(xn_ref, wg_ref, h2_ref, p_ref, wp_ref, out_ref):
    gate = jax.nn.sigmoid(jnp.dot(xn_ref[...], wg_ref[...], preferred_element_type=F32))
    proj = jnp.dot(p_ref[...].astype(BF16), wp_ref[...], preferred_element_type=F32)
    out_ref[...] = h2_ref[...] + gate * proj


def _ple(xn, h2, wg, p, wp):
    n, d = h2.shape
    tm = _tile(n, 1024)
    tn = _tile(d, 512)
    return pl.pallas_call(
        _ple_kernel,
        grid=(n // tm, d // tn),
        in_specs=[
            pl.BlockSpec((tm, d), lambda i, j: (i, 0)),
            pl.BlockSpec((d, tn), lambda i, j: (0, j)),
            pl.BlockSpec((tm, tn), lambda i, j: (i, j)),
            pl.BlockSpec((tm, p.shape[1]), lambda i, j: (i, 0)),
            pl.BlockSpec((p.shape[1], tn), lambda i, j: (0, j)),
        ],
        out_specs=pl.BlockSpec((tm, tn), lambda i, j: (i, j)),
        out_shape=jax.ShapeDtypeStruct((n, d), F32),
        compiler_params=_params(("parallel", "arbitrary"), 40),
        name="ple",
    )(xn, wg, h2, p, wp)


def _final_norm_kernel(x_ref, g_ref, o_ref):
    def norm(rs):
        o_ref[rs, :] = _rms(x_ref[rs, :], g_ref[...])

    _rows_loop(x_ref.shape[0], norm)


def _final_norm(h, g):
    n, d = h.shape
    tm = _tile(n, 512)
    return pl.pallas_call(
        _final_norm_kernel,
        grid=(n // tm,),
        in_specs=[pl.BlockSpec((tm, d), lambda i: (i, 0)), pl.BlockSpec((1, d), lambda i: (0, 0))],
        out_specs=pl.BlockSpec((tm, d), lambda i: (i, 0)),
        out_shape=jax.ShapeDtypeStruct((n, d), F32),
        compiler_params=_params(("parallel",), 40),
        name="final_norm",
    )(h, g)


def _prepare(g_mix, w_in, hgrn_lb_logits, hgrn_norm, w_gla_a2, b_gla_a, gla_norm, w_out, g_ffn,
             w_peer_q, peer_sub_keys, peer_u, peer_v, g_ple, w_ple_gate, w_ple_proj):
    depth = w_in.shape[0]
    hh, hdv = hgrn_norm.shape[1:]
    hw = hh * hdv
    low_rank = w_gla_a2.shape[1]
    main = w_in.shape[2] - low_rank
    soft = jax.nn.softmax(hgrn_lb_logits.astype(F32), axis=0)
    lb = jnp.maximum(jnp.cumsum(soft, axis=0) - soft[0], 0.0).reshape(depth, hh, 1, hw // hh)
    lb_rows = jnp.concatenate([1.0 - lb, jnp.log(lb), jnp.log1p(-lb),
                               jnp.zeros((depth, hh, 5, hw // hh), F32)], axis=2)
    w_in_bf16 = w_in.astype(BF16)
    layers = []
    for i in range(depth):
        layers.append(dict(
            g_mix=g_mix[i][None], g_ffn=g_ffn[i][None], g_ple=g_ple[i][None],
            w_in=w_in_bf16, main_cols=main,
            w_lr=jnp.pad(w_in[i, :, main:], ((0, 0), (0, LANE - low_rank))).astype(BF16),
            w_a2=jnp.pad(w_gla_a2[i], ((0, LANE - low_rank), (0, 0))).astype(BF16),
            b_a=b_gla_a[i][None],
            lb_rows=lb_rows[i],
            hgrn_norm=hgrn_norm[i][:, None, :], gla_norm=gla_norm[i][:, None, :],
            w_out_top=w_out[i, :hw].astype(BF16), w_out_bot=w_out[i, hw:].astype(BF16),
            w_peer_q=w_peer_q[i].astype(BF16), keys=peer_sub_keys[i].astype(BF16),
            u=peer_u[i].astype(BF16), v=peer_v[i].astype(BF16),
            w_ple_gate=w_ple_gate[i].astype(BF16), w_ple_proj=w_ple_proj[i].astype(BF16),
        ))
    return layers


def _trunk(x, p, st_h, st_g, layers, g_final):
    batch, seq, d = x.shape
    n = batch * seq
    h = x.reshape(n, d)
    new_h, new_g = [], []
    for i, lw in enumerate(layers):
        hh, _, hdk = lw["hgrn_norm"].shape
        gh, _, gdv = lw["gla_norm"].shape
        hw = hh * hdk
        gkw = lw["w_a2"].shape[1]
        gdk = gkw // gh
        z3, la3 = _in_proj(h, lw["g_mix"], lw["w_in"], i, lw["main_cols"], lw["w_lr"],
                           lw["w_a2"], lw["b_a"])
        s_hw = hw // LANE
        o_h, s_h = _recurrence(
            z3, lw["lb_rows"], lw["hgrn_norm"], None if st_h is None else st_h[i],
            mode="hgrn", batch=batch, seq=seq, heads=hh, dk=hdk, dv=hdk,
            q_off=0, k_off=s_hw, v_off=2 * s_hw, r_off=3 * s_hw, aux_is_slab=False,
            heads_per_step=8)
        g0 = 4 * s_hw
        s_kw = gkw // LANE
        s_vw = gh * gdv // LANE
        o_g, s_g = _recurrence(
            z3, la3, lw["gla_norm"], None if st_g is None else st_g[i],
            mode="gla", batch=batch, seq=seq, heads=gh, dk=gdk, dv=gdv,
            q_off=g0, k_off=g0 + s_kw, v_off=g0 + 2 * s_kw, r_off=g0 + 2 * s_kw + s_vw,
            aux_is_slab=True, heads_per_step=4)
        h = _out_proj(o_h, o_g, lw["w_out_top"], lw["w_out_bot"], h)
        xn, s1, e1, tr, er = _peer_scores(h, lw["g_ffn"], lw["w_peer_q"], lw["keys"])
        y = _peer_dense(xn, lw["u"], lw["v"], s1, e1, tr, er)
        h2, xn = _add_norm(h, y, lw["g_ple"])
        h = _ple(xn, h2, lw["w_ple_gate"], p[i].reshape(n, -1), lw["w_ple_proj"])
        new_h.append(s_h)
        new_g.append(s_g)
    out = _final_norm(h, g_final[None])
    return out.reshape(batch, seq, d), jnp.stack(new_h), jnp.stack(new_g)


def kernel(x_prompt, x_sample, state_hgrn, state_gla, p_prompt, p_sample, g_mix, w_in,
           hgrn_lb_logits, hgrn_norm, w_gla_a2, b_gla_a, gla_norm, w_out, g_ffn, w_peer_q,
           peer_sub_keys, peer_u, peer_v, g_ple, w_ple_gate, w_ple_proj, g_final):
    layers = _prepare(g_mix, w_in, hgrn_lb_logits, hgrn_norm, w_gla_a2, b_gla_a, gla_norm, w_out,
                      g_ffn, w_peer_q, peer_sub_keys, peer_u, peer_v, g_ple, w_ple_gate,
                      w_ple_proj)
    y_p, h_p, g_p = _trunk(x_prompt, p_prompt, None, None, layers, g_final)
    y_s, h_s, g_s = _trunk(x_sample, p_sample, state_hgrn, state_gla, layers, g_final)
    return y_p, y_s, h_p, g_p, h_s, g_s
```

```python
import functools

import jax
import jax.numpy as jnp
from jax import lax
from jax.experimental import pallas as pl
from jax.experimental.pallas import tpu as pltpu

F32 = jnp.float32
BF16 = jnp.bfloat16
EPS = 1e-6
LANE = 128
SUB = 16
CHUNK = 64
TOPK = 16
GATE_NORMALIZER = 16.0
GROUP = 2
NEG_INF = float("-inf")
LOG2E = 1.4426950408889634
MIB = 1024 * 1024

_NT = (((1,), (1,)), ((), ()))
_TN = (((0,), (0,)), ((), ()))


def _params(sem, vmem_mib):
    return pltpu.CompilerParams(dimension_semantics=sem, vmem_limit_bytes=vmem_mib * MIB)


def _rms(x, g):
    return x * lax.rsqrt(jnp.mean(x * x, axis=-1, keepdims=True) + EPS) * g


def _log1p_exp_neg(x):
    return jnp.log(1.0 + jnp.exp(-x))


def _log_sigmoid(x):
    return jnp.minimum(x, 0.0) - _log1p_exp_neg(jnp.abs(x))


def _tile(n, pref):
    return pref if n % pref == 0 else n


def _rows_loop(n_rows, fn, rows=128):
    rows = min(rows, n_rows)

    def body(i, carry):
        fn(pl.ds(pl.multiple_of(i * rows, rows), rows))
        return carry

    lax.fori_loop(0, n_rows // rows, body, 0)


def _in_proj_kernel(x_ref, g_ref, w_ref, wlr_ref, wa2_ref, ba_ref, z_ref, la_ref, xn_ref):
    @pl.when(pl.program_id(1) == 0)
    def _():
        def norm(rs):
            xn_ref[rs, :] = _rms(x_ref[rs, :], g_ref[...]).astype(BF16)

        _rows_loop(x_ref.shape[0], norm)
        glr = jnp.dot(xn_ref[...], wlr_ref[...], preferred_element_type=F32)
        ga = jnp.dot(glr.astype(BF16), wa2_ref[...], preferred_element_type=F32) + ba_ref[...]
        la = _log_sigmoid(ga) * (1.0 / GATE_NORMALIZER)
        for s in range(la_ref.shape[0]):
            la_ref[s] = la[:, s * LANE:(s + 1) * LANE]

    res = jnp.dot(xn_ref[...], w_ref[...], preferred_element_type=F32)
    for s in range(z_ref.shape[0]):
        z_ref[s] = res[:, s * LANE:(s + 1) * LANE]


def _in_proj(h, g, w_all, layer, cols, w_lr, w_a2, b_a):
    n, d = h.shape
    kw = w_a2.shape[1]
    tm = _tile(n, 512)
    tn = _tile(cols, 1024)
    return pl.pallas_call(
        _in_proj_kernel,
        grid=(n // tm, cols // tn),
        in_specs=[
            pl.BlockSpec((tm, d), lambda i, j: (i, 0)),
            pl.BlockSpec((1, d), lambda i, j: (0, 0)),
            pl.BlockSpec((None, d, tn), lambda i, j: (layer, 0, j)),
            pl.BlockSpec((d, LANE), lambda i, j: (0, 0)),
            pl.BlockSpec((LANE, kw), lambda i, j: (0, 0)),
            pl.BlockSpec((1, kw), lambda i, j: (0, 0)),
        ],
        out_specs=[
            pl.BlockSpec((tn // LANE, tm, LANE), lambda i, j: (j, i, 0)),
            pl.BlockSpec((kw // LANE, tm, LANE), lambda i, j: (0, i, 0)),
        ],
        out_shape=[
            jax.ShapeDtypeStruct((cols // LANE, n, LANE), F32),
            jax.ShapeDtypeStruct((kw // LANE, n, LANE), F32),
        ],
        scratch_shapes=[pltpu.VMEM((tm, d), BF16)],
        compiler_params=_params(("parallel", "arbitrary"), 52),
        name="in_proj",
    )(h, g, w_all, w_lr, w_a2, b_a)


def _head_lanes(x, hd, hb):
    w = x.shape[1] // hb
    zero = jnp.zeros((x.shape[0], w), x.dtype)
    return jnp.concatenate([x[:, j * w:(j + 1) * w] if j == hd else zero for j in range(hb)],
                           axis=1)


def _rows_per_block(ref, s, nsub):
    return jnp.concatenate([jnp.broadcast_to(ref[pl.ds(j * SUB + s, 1), :], (SUB, ref.shape[1]))
                            for j in range(nsub)], axis=0)


def _chunk_step(q, k, v, g, ones_bd, st_ref, b_ref, k_ref, c, log2k=None):
    hb, dv, dk = st_ref.shape
    nsub = c // SUB
    row = lax.broadcasted_iota(jnp.int32, (c, c), 0)
    col = lax.broadcasted_iota(jnp.int32, (c, c), 1)
    tril = jnp.where(row >= col, 1.0, 0.0).astype(F32)
    b = jnp.dot(tril, g * LOG2E, precision=lax.Precision.HIGHEST, preferred_element_type=F32)
    b_ref[...] = b
    k_ref[...] = k if log2k is None else log2k - b
    vb = v.astype(BF16)
    b_last = b_ref[pl.ds(c - 1, 1), :]
    a_rows = [jnp.zeros((SUB, hb * c), F32)]
    for blk in range(1, nsub):
        lo = blk * SUB
        ref = b_ref[pl.ds(lo - 1, 1), :]
        q_dec = (q[lo:lo + SUB] * jnp.exp2(b[lo:lo + SUB] - ref)).astype(BF16)
        k_dec = jnp.concatenate(
            [(k[:lo] * jnp.exp2(ref - b[:lo])).astype(BF16), jnp.zeros((c - lo, hb * dk), BF16)],
            axis=0)
        k_bd = jnp.concatenate([_head_lanes(k_dec, hd, hb) for hd in range(hb)], axis=0)
        a_rows.append(lax.dot_general(q_dec, k_bd, _NT, preferred_element_type=F32))
    a = jnp.concatenate(a_rows, axis=0)
    prods = []
    for s in range(SUB):
        if log2k is None:
            k_s = _rows_per_block(k_ref, s, nsub)
            prods.append((q * k_s * jnp.exp2(b - _rows_per_block(b_ref, s, nsub))).astype(BF16))
        else:
            prods.append((q * jnp.exp2(b + _rows_per_block(k_ref, s, nsub))).astype(BF16))
    sums = jnp.dot(jnp.concatenate(prods, axis=0), ones_bd, preferred_element_type=F32)
    row_a = lax.broadcasted_iota(jnp.int32, (c, hb * c), 0)
    col_a = lax.broadcasted_iota(jnp.int32, (c, hb * c), 1) % c
    block_start = (row_a // SUB) * SUB
    for s in range(SUB):
        a = jnp.where(col_a == block_start + s, sums[s * c:(s + 1) * c], a)
    a = jnp.where(col_a <= row_a, a, 0.0)
    v_bd = jnp.concatenate([_head_lanes(vb, hd, hb) for hd in range(hb)], axis=0)
    o = jnp.dot(a.astype(BF16), v_bd, preferred_element_type=F32)
    q_abs = (q * jnp.exp2(b)).astype(BF16)
    k_end = (k * jnp.exp2(b_last - b)).astype(BF16)
    decay = jnp.exp2(b_last)
    from_state = []
    for hd in range(hb):
        keys = slice(hd * dk, (hd + 1) * dk)
        st = st_ref[hd]
        from_state.append(lax.dot_general(q_abs[:, keys], st.astype(BF16), _NT,
                                          preferred_element_type=F32))
        st_ref[hd] = st * decay[:, keys] + lax.dot_general(
            vb[:, hd * dv:(hd + 1) * dv], k_end[:, keys], _TN, preferred_element_type=F32)
    return o + jnp.concatenate(from_state, axis=1)


def _slabs(ref, r0, c):
    return jnp.concatenate([ref[s, pl.ds(r0, c), :] for s in range(ref.shape[0])], axis=-1)


def _rec_kernel(*refs, mode, c, nchunk, has_state):
    if has_state:
        (q_ref, k_ref, v_ref, r_ref, aux_ref, gain_ref, ones_ref, s0_ref,
         o_ref, sout_ref, st_ref, b_sc, k_sc) = refs
    else:
        (q_ref, k_ref, v_ref, r_ref, aux_ref, gain_ref, ones_ref,
         o_ref, sout_ref, st_ref, b_sc, k_sc) = refs
        s0_ref = None
    t = pl.program_id(2)
    hb, dv, dk = st_ref.shape

    @pl.when(t == 0)
    def _():
        for hd in range(hb):
            if has_state:
                st_ref[hd] = s0_ref[0, hd].T
            else:
                st_ref[hd] = jnp.zeros((dv, dk), F32)

    def chunk(ci, carry):
        r0 = pl.multiple_of(ci * c, c)
        xq = _slabs(q_ref, r0, c)
        xk = _slabs(k_ref, r0, c)
        xv = _slabs(v_ref, r0, c)
        xr = _slabs(r_ref, r0, c)
        if mode == "hgrn":
            one_m_lb, log_lb, log_1m_lb = (
                jnp.concatenate([aux_ref[hd, r:r + 1, :] for hd in range(hb)], axis=1)
                for r in range(3))
            q = xq * jax.nn.sigmoid(xq)
            k = one_m_lb * jax.nn.sigmoid(-xk)
            gated = log_1m_lb + _log_sigmoid(xk)
            g = jnp.maximum(log_lb, gated) + _log1p_exp_neg(jnp.abs(log_lb - gated))
            log2k = (gated - xk) * LOG2E
            out_gate = jax.nn.sigmoid(xr)
        else:
            q = xq * (dk ** -0.5)
            k = xk
            g = _slabs(aux_ref, r0, c)
            log2k = None
            out_gate = xr * jax.nn.sigmoid(xr)
        o = _chunk_step(q, k, xv, g, ones_ref[...], st_ref, b_sc, k_sc, c, log2k)
        for hd in range(hb):
            lanes = slice(hd * dv, (hd + 1) * dv)
            o_ref[pl.ds(r0, c), lanes] = (
                _rms(o[:, lanes], gain_ref[hd]) * out_gate[:, lanes]).astype(o_ref.dtype)
        return carry

    lax.fori_loop(0, nchunk, chunk, 0)

    @pl.when(t == pl.num_programs(2) - 1)
    def _():
        for hd in range(hb):
            sout_ref[0, hd] = st_ref[hd].T


def _recurrence(z3, aux, gain, s0, *, mode, batch, seq, heads, dk, dv, q_off, k_off, v_off, r_off,
                aux_is_slab, heads_per_step):
    c = min(CHUNK, seq)
    tb = _tile(seq, 256) if seq >= 256 else seq
    nchunk = tb // c
    nt = seq // tb
    kq, kv = dk // LANE, dv // LANE
    has_state = s0 is not None
    hb = min(heads, heads_per_step)
    assert heads % hb == 0

    def slab_spec(per_head, off):
        width = per_head * hb
        assert off % width == 0
        return pl.BlockSpec((width, tb, LANE), lambda b, h, t: (off // width + h, b * nt + t, 0))

    in_specs = [slab_spec(kq, q_off), slab_spec(kq, k_off), slab_spec(kv, v_off),
                slab_spec(kv, r_off)]
    if aux_is_slab:
        in_specs.append(slab_spec(kq, 0))
    else:
        in_specs.append(pl.BlockSpec((hb, 8, dk), lambda b, h, t: (h, 0, 0)))
    in_specs.append(pl.BlockSpec((hb, 1, dv), lambda b, h, t: (h, 0, 0)))
    in_specs.append(pl.BlockSpec((hb * dk, hb * c), lambda b, h, t: (0, 0)))
    ones_bd = jnp.kron(jnp.eye(hb, dtype=F32), jnp.ones((dk, c), F32)).astype(BF16)
    args = [z3, z3, z3, z3, aux, gain, ones_bd]
    if has_state:
        in_specs.append(pl.BlockSpec((1, hb, dk, dv), lambda b, h, t: (b, h, 0, 0)))
        args.append(s0)
    kern = functools.partial(_rec_kernel, mode=mode, c=c, nchunk=nchunk, has_state=has_state)
    return pl.pallas_call(
        kern,
        grid=(batch, heads // hb, nt),
        in_specs=in_specs,
        out_specs=[
            pl.BlockSpec((tb, hb * dv), lambda b, h, t: (b * nt + t, h)),
            pl.BlockSpec((1, hb, dk, dv), lambda b, h, t: (b, h, 0, 0)),
        ],
        out_shape=[
            jax.ShapeDtypeStruct((batch * seq, heads * dv), BF16),
            jax.ShapeDtypeStruct((batch, heads, dk, dv), F32),
        ],
        scratch_shapes=[pltpu.VMEM((hb, dv, dk), F32), pltpu.VMEM((c, hb * dk), F32),
                        pltpu.VMEM((c, hb * dk), F32)],
        compiler_params=_params(("parallel", "parallel", "arbitrary"), 40),
        name=mode + "_recurrence",
    )(*args)


def _out_proj_kernel(oh_ref, og_ref, w1_ref, w2_ref, h_ref, out_ref):
    acc = jnp.dot(oh_ref[...], w1_ref[...], preferred_element_type=F32)
    acc = acc + jnp.dot(og_ref[...], w2_ref[...], preferred_element_type=F32)
    out_ref[...] = h_ref[...] + acc


def _out_proj(o_h, o_g, w_top, w_bot, h):
    n, d = h.shape
    tm = _tile(n, 512)
    tn = _tile(d, 1024)
    return pl.pallas_call(
        _out_proj_kernel,
        grid=(n // tm, d // tn),
        in_specs=[
            pl.BlockSpec((tm, o_h.shape[1]), lambda i, j: (i, 0)),
            pl.BlockSpec((tm, o_g.shape[1]), lambda i, j: (i, 0)),
            pl.BlockSpec((w_top.shape[0], tn), lambda i, j: (0, j)),
            pl.BlockSpec((w_bot.shape[0], tn), lambda i, j: (0, j)),
            pl.BlockSpec((tm, tn), lambda i, j: (i, j)),
        ],
        out_specs=pl.BlockSpec((tm, tn), lambda i, j: (i, j)),
        out_shape=jax.ShapeDtypeStruct((n, d), F32),
        compiler_params=_params(("parallel", "arbitrary"), 40),
        name="out_proj",
    )(o_h, o_g, w_top, w_bot, h)


def _sorting_network(n):
    pairs = []
    p = 1
    while p < n:
        k = p
        while k >= 1:
            for j in range(k % p, n - k, 2 * k):
                for i in range(min(k, n - j - k)):
                    if (i + j) // (2 * p) == (i + j + k) // (2 * p):
                        pairs.append((i + j, i + j + k))
            k //= 2
        p *= 2
    return pairs


def _top_values(x):
    groups = x.shape[0] // 8
    lists = [x[8 * v:8 * (v + 1)] for v in range(groups)]
    for i, j in _sorting_network(groups):
        hi, lo = jnp.maximum(lists[i], lists[j]), jnp.minimum(lists[i], lists[j])
        lists[i], lists[j] = hi, lo
    rows = lax.broadcasted_iota(jnp.int32, (TOPK, LANE), 0)
    vals = jnp.full((TOPK, LANE), NEG_INF, F32)
    cnts = jnp.zeros((TOPK, LANE), F32)
    for a in range(TOPK):
        m = jnp.max(lists[0], axis=0, keepdims=True)
        eq = lists[0] == m
        n_eq = jnp.sum(jnp.where(eq, 1.0, 0.0), axis=0, keepdims=True)
        vals = jnp.where(rows == a, m, vals)
        cnts = jnp.where(rows == a, n_eq, cnts)
        depth = min(groups, TOPK - a) - 1
        for v in range(depth):
            lists[v] = jnp.where(eq, lists[v + 1], lists[v])
        if depth < groups:
            lists[depth] = jnp.where(eq, NEG_INF, lists[depth])
    return vals, cnts


def _candidate_sums(v0, n0, v1, n1):
    half = TOPK // 2
    r16 = lax.broadcasted_iota(jnp.int32, (TOPK, LANE), 0)
    r8 = lax.broadcasted_iota(jnp.int32, (half, LANE), 0)
    pieces = [
        (v0[0:1] + v1, n0[0:1] * n1, None),
        (v0[1:2] + v1[:half], n0[1:2] * n1[:half], None),
        (v0[2:3] + v1[:half], n0[2:3] * n1[:half], r8 < 5),
        (v0[3:4] + v1[:half], n0[3:4] * n1[:half], r8 < 4),
        (v0 + v1[0:1], n0 * n1[0:1], r16 >= 4),
        (v0[:half] + v1[1:2], n0[:half] * n1[1:2], r8 >= 4),
        (v0[:half] + v1[2:3], n0[:half] * n1[2:3], r8 == 4),
    ]
    cand = jnp.concatenate([c if m is None else jnp.where(m, c, NEG_INF) for c, _, m in pieces],
                           axis=0)
    mult = jnp.concatenate([n if m is None else jnp.where(m, n, 0.0) for _, n, m in pieces],
                           axis=0)
    return cand, mult


def _peer_score_kernel(h_ref, g_ref, wq_ref, keys_ref, xn_ref, s1_ref, e1_ref, tr_ref, er_ref,
                       sc_ref):
    @pl.when(pl.program_id(1) == 0)
    def _():
        def norm(rs):
            xn_ref[rs, :] = _rms(h_ref[rs, :], g_ref[...]).astype(BF16)

        _rows_loop(h_ref.shape[0], norm)

    pq =jnp.dot(xn_ref[...], wq_ref[...], preferred_element_type=F32)
    for cset in range(2):
        qc = pq[:, cset * LANE:(cset + 1) * LANE].astype(BF16)
        sc_ref[cset] = lax.dot_general(keys_ref[cset], qc, _NT, preferred_element_type=F32)

    def lanes(ci, carry):
        l0 = pl.multiple_of(ci * LANE, LANE)
        s0 = sc_ref[0, :, pl.ds(l0, LANE)]
        s1 = sc_ref[1, :, pl.ds(l0, LANE)]
        v0, n0 = _top_values(s0)
        v1, n1 = _top_values(s1)
        cand, mult = _candidate_sums(v0, n0, v1, n1)
        top = v0[0:1] + v1[0:1]
        tau = jnp.full((1, LANE), NEG_INF, F32)
        seen = jnp.zeros((1, LANE), F32)
        x = cand
        for _ in range(TOPK):
            m = jnp.max(x, axis=0, keepdims=True)
            eq = x == m
            tau = jnp.where(seen < TOPK, m, tau)
            seen = seen + jnp.sum(jnp.where(eq, mult, 0.0), axis=0, keepdims=True)
            x = jnp.where(eq, NEG_INF, x)
        z = jnp.sum(jnp.where(cand >= tau, mult * jnp.exp(cand - top), 0.0), axis=0,
                    keepdims=True)
        s1_ref[0, :, pl.ds(l0, LANE)] = s1
        e1_ref[0, :, pl.ds(l0, LANE)] = jnp.exp(s1 - v1[0:1])
        tr_ref[0, :, pl.ds(l0, LANE)] = tau - s0
        er_ref[0, :, pl.ds(l0, LANE)] = jnp.exp(s0 - v0[0:1]) / z
        return carry

    lax.fori_loop(0, sc_ref.shape[2] // LANE, lanes, 0)


def _peer_scores(h, g, wq, keys):
    n, d = h.shape
    heads = wq.shape[1] // (2 * LANE)
    tm = _tile(n, 512)
    aux_spec = pl.BlockSpec((1, LANE, tm), lambda i, hd: (hd, 0, i))
    aux_shape = jax.ShapeDtypeStruct((heads, LANE, n), F32)
    return pl.pallas_call(
        _peer_score_kernel,
        grid=(n // tm, heads),
        in_specs=[
            pl.BlockSpec((tm, d), lambda i, hd: (i, 0)),
            pl.BlockSpec((1, d), lambda i, hd: (0, 0)),
            pl.BlockSpec((d, 2 * LANE), lambda i, hd: (0, hd)),
            pl.BlockSpec((2, LANE, LANE), lambda i, hd: (0, 0, 0)),
        ],
        out_specs=[pl.BlockSpec((tm, d), lambda i, hd: (i, 0)), aux_spec, aux_spec, aux_spec,
                   aux_spec],
        out_shape=[jax.ShapeDtypeStruct((n, d), BF16), aux_shape, aux_shape, aux_shape,
                   aux_shape],
        scratch_shapes=[pltpu.VMEM((2, LANE, tm), F32)],
        compiler_params=_params(("parallel", "arbitrary"), 40),
        name="peer_scores",
    )(h, g, wq, keys)


def _peer_dense_kernel(xn_ref, u_ref, v_ref, s1_ref, e1_ref, tr_ref, er_ref, y_ref):
    e = pl.program_id(1)
    heads = s1_ref.shape[0]
    tiles = u_ref.shape[0] // LANE

    @pl.when(e == 0)
    def _():
        y_ref[...] = jnp.zeros(y_ref.shape, F32)

    xn = xn_ref[...]
    gated = []
    for grp in range(tiles // GROUP):
        rows = slice(grp * GROUP * LANE, (grp + 1) * GROUP * LANE)
        act = lax.dot_general(xn, u_ref[rows, :], _NT, preferred_element_type=F32)
        for ii in range(GROUP):
            i = e * tiles + grp * GROUP + ii
            wt = jnp.zeros((LANE, xn_ref.shape[0]), F32)
            for hd in range(heads):
                thr = tr_ref[hd, pl.ds(i, 1), :]
                scale = er_ref[hd, pl.ds(i, 1), :]
                wt = wt + jnp.where(s1_ref[hd] >= thr, e1_ref[hd] * scale, 0.0)
            a = act[:, ii * LANE:(ii + 1) * LANE]
            gelu = 0.5 * a * (1.0 + lax.erf(a * 0.7071067811865476))
            gated.append((wt.T * gelu).astype(BF16))
    y_ref[...] += jnp.dot(jnp.concatenate(gated, axis=1), v_ref[...],
                          preferred_element_type=F32)


def _peer_dense(xn, u, v, layer, s1, e1, tr, er):
    n, d = xn.shape
    n_exp = u.shape[1]
    heads = s1.shape[0]
    tn = _tile(n, 512)
    te = 512
    once = pl.Buffered(1)
    aux_spec = pl.BlockSpec((heads, LANE, tn), lambda t, e: (0, 0, t), pipeline_mode=once)
    return pl.pallas_call(
        _peer_dense_kernel,
        grid=(n // tn, n_exp // te),
        in_specs=[
            pl.BlockSpec((tn, d), lambda t, e: (t, 0), pipeline_mode=once),
            pl.BlockSpec((None, te, d), lambda t, e: (layer, e, 0)),
            pl.BlockSpec((None, te, d), lambda t, e: (layer, e, 0)),
            aux_spec, aux_spec, aux_spec, aux_spec,
        ],
        out_specs=pl.BlockSpec((tn, d), lambda t, e: (t, 0)),
        out_shape=jax.ShapeDtypeStruct((n, d), F32),
        compiler_params=_params(("parallel", "arbitrary"), 56),
        name="peer_dense",
    )(xn, u, v, s1, e1, tr, er)


def _add_norm_kernel(h_ref, y_ref, g_ref, h2_ref, xn_ref):
    def norm(rs):
        h2 = h_ref[rs, :] + y_ref[rs, :]
        h2_ref[rs, :] = h2
        xn_ref[rs, :] = _rms(h2, g_ref[...]).astype(BF16)

    _rows_loop(h_ref.shape[0], norm)


def _add_norm(h, y, g):
    n, d = h.shape
    tm = _tile(n, 256)
    row = pl.BlockSpec((tm, d), lambda i: (i, 0))
    return pl.pallas_call(
        _add_norm_kernel,
        grid=(n // tm,),
        in_specs=[row, row, pl.BlockSpec((1, d), lambda i: (0, 0))],
        out_specs=[row, row],
        out_shape=[jax.ShapeDtypeStruct((n, d), F32), jax.ShapeDtypeStruct((n, d), BF16)],
        compiler_params=_params(("parallel",), 40),
        name="add_norm",
    )(h, y, g)


def _ple_kernel(xn_ref, wg_ref, h2_ref, p_ref, wp_ref, out_ref):
    gate = jax.nn.sigmoid(jnp.dot(xn_ref[...], wg_ref[...], preferred_element_type=F32))
    proj = jnp.dot(p_ref[...].astype(BF16), wp_ref[...], preferred_element_type=F32)
    out_ref[...] = h2_ref[...] + gate * proj


def _ple(xn, h2, wg, p, wp):
    n, d = h2.shape
    tm = _tile(n, 1024)
    tn = _tile(d, 512)
    return pl.pallas_call(
        _ple_kernel,
        grid=(n // tm, d // tn),
        in_specs=[
            pl.BlockSpec((tm, d), lambda i, j: (i, 0)),
            pl.BlockSpec((d, tn), lambda i, j: (0, j)),
            pl.BlockSpec((tm, tn), lambda i, j: (i, j)),
            pl.BlockSpec((tm, p.shape[1]), lambda i, j: (i, 0)),
            pl.BlockSpec((p.shape[1], tn), lambda i, j: (0, j)),
        ],
        out_specs=pl.BlockSpec((tm, tn), lambda i, j: (i, j)),
        out_shape=jax.ShapeDtypeStruct((n, d), F32),
        compiler_params=_params(("parallel", "arbitrary"), 40),
        name="ple",
    )(xn, wg, h2, p, wp)


def _final_norm_kernel(x_ref, g_ref, o_ref):
    def norm(rs):
        o_ref[rs, :] = _rms(x_ref[rs, :], g_ref[...])

    _rows_loop(x_ref.shape[0], norm)


def _final_norm(h, g):
    n, d = h.shape
    tm = _tile(n, 512)
    return pl.pallas_call(
        _final_norm_kernel,
        grid=(n // tm,),
        in_specs=[pl.BlockSpec((tm, d), lambda i: (i, 0)), pl.BlockSpec((1, d), lambda i: (0, 0))],
        out_specs=pl.BlockSpec((tm, d), lambda i: (i, 0)),
        out_shape=jax.ShapeDtypeStruct((n, d), F32),
        compiler_params=_params(("parallel",), 40),
        name="final_norm",
    )(h, g)


def _prepare(g_mix, w_in, hgrn_lb_logits, hgrn_norm, w_gla_a2, b_gla_a, gla_norm, w_out, g_ffn,
             w_peer_q, peer_sub_keys, peer_u, peer_v, g_ple, w_ple_gate, w_ple_proj):
    depth = w_in.shape[0]
    hh, hdv = hgrn_norm.shape[1:]
    hw = hh * hdv
    low_rank = w_gla_a2.shape[1]
    main = w_in.shape[2] - low_rank
    soft = jax.nn.softmax(hgrn_lb_logits.astype(F32), axis=0)
    lb = jnp.maximum(jnp.cumsum(soft, axis=0) - soft[0], 0.0).reshape(depth, hh, 1, hw // hh)
    lb_rows = jnp.concatenate([1.0 - lb, jnp.log(lb), jnp.log1p(-lb),
                               jnp.zeros((depth, hh, 5, hw // hh), F32)], axis=2)
    w_in_bf16 = w_in.astype(BF16)
    u_bf16 = peer_u.astype(BF16)
    v_bf16 = peer_v.astype(BF16)
    layers = []
    for i in range(depth):
        layers.append(dict(
            g_mix=g_mix[i][None], g_ffn=g_ffn[i][None], g_ple=g_ple[i][None],
            w_in=w_in_bf16, main_cols=main,
            w_lr=jnp.pad(w_in[i, :, main:], ((0, 0), (0, LANE - low_rank))).astype(BF16),
            w_a2=jnp.pad(w_gla_a2[i], ((0, LANE - low_rank), (0, 0))).astype(BF16),
            b_a=b_gla_a[i][None],
            lb_rows=lb_rows[i],
            hgrn_norm=hgrn_norm[i][:, None, :], gla_norm=gla_norm[i][:, None, :],
            w_out_top=w_out[i, :hw].astype(BF16), w_out_bot=w_out[i, hw:].astype(BF16),
            w_peer_q=w_peer_q[i].astype(BF16), keys=peer_sub_keys[i].astype(BF16),
            u=u_bf16, v=v_bf16,
            w_ple_gate=w_ple_gate[i].astype(BF16), w_ple_proj=w_ple_proj[i].astype(BF16),
        ))
    return layers


def _trunk(x, p, st_h, st_g, layers, g_final):
    batch, seq, d = x.shape
    n = batch * seq
    h = x.reshape(n, d)
    new_h, new_g = [], []
    for i, lw in enumerate(layers):
        hh, _, hdk = lw["hgrn_norm"].shape
        gh, _, gdv = lw["gla_norm"].shape
        hw = hh * hdk
        gkw = lw["w_a2"].shape[1]
        gdk = gkw // gh
        z3, la3 = _in_proj(h, lw["g_mix"], lw["w_in"], i, lw["main_cols"], lw["w_lr"],
                           lw["w_a2"], lw["b_a"])
        s_hw = hw // LANE
        o_h, s_h = _recurrence(
            z3, lw["lb_rows"], lw["hgrn_norm"], None if st_h is None else st_h[i],
            mode="hgrn", batch=batch, seq=seq, heads=hh, dk=hdk, dv=hdk,
            q_off=0, k_off=s_hw, v_off=2 * s_hw, r_off=3 * s_hw, aux_is_slab=False,
            heads_per_step=8)
        g0 = 4 * s_hw
        s_kw = gkw // LANE
        s_vw = gh * gdv // LANE
        o_g, s_g = _recurrence(
            z3, la3, lw["gla_norm"], None if st_g is None else st_g[i],
            mode="gla", batch=batch, seq=seq, heads=gh, dk=gdk, dv=gdv,
            q_off=g0, k_off=g0 + s_kw, v_off=g0 + 2 * s_kw, r_off=g0 + 2 * s_kw + s_vw,
            aux_is_slab=True, heads_per_step=4)
        h = _out_proj(o_h, o_g, lw["w_out_top"], lw["w_out_bot"], h)
        xn, s1, e1, tr, er = _peer_scores(h, lw["g_ffn"], lw["w_peer_q"], lw["keys"])
        y = _peer_dense(xn, lw["u"], lw["v"], i, s1, e1, tr, er)
        h2, xn = _add_norm(h, y, lw["g_ple"])
        h = _ple(xn, h2, lw["w_ple_gate"], p[i].reshape(n, -1), lw["w_ple_proj"])
        new_h.append(s_h)
        new_g.append(s_g)
    out = _final_norm(h, g_final[None])
    return out.reshape(batch, seq, d), jnp.stack(new_h), jnp.stack(new_g)


def kernel(x_prompt, x_sample, state_hgrn, state_gla, p_prompt, p_sample, g_mix, w_in,
           hgrn_lb_logits, hgrn_norm, w_gla_a2, b_gla_a, gla_norm, w_out, g_ffn, w_peer_q,
           peer_sub_keys, peer_u, peer_v, g_ple, w_ple_gate, w_ple_proj, g_final):
    layers = _prepare(g_mix, w_in, hgrn_lb_logits, hgrn_norm, w_gla_a2, b_gla_a, gla_norm, w_out,
                      g_ffn, w_peer_q, peer_sub_keys, peer_u, peer_v, g_ple, w_ple_gate,
                      w_ple_proj)
    y_p, h_p, g_p = _trunk(x_prompt, p_prompt, None, None, layers, g_final)
    y_s, h_s, g_s = _trunk(x_sample, p_sample, state_hgrn, state_gla, layers, g_final)
    return y_p, y_s, h_p, g_p, h_s, g_s
```
